```python
import math
import jax, jax.numpy as jnp
from jax import lax
import numpy as np

D_MODEL = 1024
BATCH = 4
SEQ = 8192
DEPTH = 4

N_MIXERS = 4
CONV_WIDTH = 3
CONV_DIM = D_MODEL
DIL_WINDOWS = (128, 512, 2048)
DIL_RATES = (1, 4, 16)
DIL_HEADS = 8
DIL_HEAD_DIM = D_MODEL // DIL_HEADS
DIL_DIM = DIL_HEADS * DIL_HEAD_DIM
SWA_HALF = 128
SWA_Q_HEADS = 16
SWA_KV_HEADS = 4
SWA_HEAD_DIM = 64
SWA_DIM = SWA_Q_HEADS * SWA_HEAD_DIM
DIFF_HEADS = 8
DIFF_HEAD_DIM = D_MODEL // (2 * DIFF_HEADS)
DIFF_DIM = DIFF_HEADS * 2 * DIFF_HEAD_DIM
Q_BLOCK = 128

ROPE_THETA = 10000.0
NORM_EPS = 1e-6
SUBLN_EPS = 1e-5
NEG_INF = -1e30

kernel_name = "hybrid_interleaved_bidir_encoder"


def _layers_of_type(t):
    return len(range(t, DEPTH, N_MIXERS))


def rms_norm(x, g, eps):
    xf = x.astype(jnp.float32)
    y = xf * lax.rsqrt(jnp.mean(xf * xf, axis=-1, keepdims=True) + eps)
    return (y * g.astype(jnp.float32)).astype(x.dtype)


def rope_tables(seq, dim):
    inv = ROPE_THETA ** (-jnp.arange(0, dim, 2, dtype=jnp.float32) / dim)
    ang = jnp.arange(seq, dtype=jnp.float32)[:, None] * inv[None, :]
    return jnp.cos(ang), jnp.sin(ang)


def apply_rope(x, cos, sin):
    shp = (cos.shape[0],) + (1,) * (x.ndim - 3) + (cos.shape[1],)
    cos, sin = cos.reshape(shp), sin.reshape(shp)
    x1, x2 = jnp.split(x.astype(jnp.float32), 2, axis=-1)
    return jnp.concatenate([x1 * cos - x2 * sin, x2 * cos + x1 * sin], axis=-1).astype(x.dtype)


def adaln_pre_norm(x, c, g, w_mod, b_mod):
    mod = jax.nn.silu(c) @ w_mod + b_mod
    shift, scale, gate = jnp.split(mod, 3, axis=-1)
    h = rms_norm(x, g, NORM_EPS) * (1.0 + scale[:, None, :]) + shift[:, None, :]
    return h, gate


def banded_attention(q, k, v, half, sink=None):
    N, L, Hk, G, D = q.shape
    Dv = v.shape[-1]
    blk = half
    nb = -(-L // blk)
    Lp = nb * blk
    qp = jnp.pad(q, ((0, 0), (0, Lp - L), (0, 0), (0, 0), (0, 0))).reshape(N, nb, blk, Hk, G, D)
    kv_pad = ((0, 0), (blk, Lp - L + blk), (0, 0), (0, 0))
    kp = jnp.pad(k, kv_pad).reshape(N, nb + 2, blk, Hk, D)
    vp = jnp.pad(v, kv_pad).reshape(N, nb + 2, blk, Hk, Dv)
    band = lambda t: jnp.concatenate([t[:, :-2], t[:, 1:-1], t[:, 2:]], axis=2)
    kw, vw = band(kp), band(vp)
    qpos = jnp.arange(Lp).reshape(nb, blk)
    kpos = (jnp.arange(nb)[:, None] - 1) * blk + jnp.arange(3 * blk)[None, :]
    kk = kpos[:, None, :]
    valid = (jnp.abs(kk - qpos[:, :, None]) <= half) & (kk >= 0) & (kk < L)
    scale = D ** -0.5
    sink_f = None if sink is None else sink.astype(jnp.float32)[None, :, :, None]

    def one(args):
        qs, ks, vs = args
        s = jnp.einsum('bqhgd,bkhd->bhgqk', qs, ks, preferred_element_type=jnp.float32) * scale
        s = jnp.where(valid[:, None, None], s, NEG_INF)
        m = jnp.max(s, axis=-1)
        if sink_f is not None:
            m = jnp.maximum(m, sink_f)
        p = jnp.exp(s - m[..., None])
        l = jnp.sum(p, axis=-1)
        if sink_f is not None:
            l = l + jnp.exp(sink_f - m)
        o = jnp.einsum('bhgqk,bkhe->bqhge', p, vs.astype(jnp.float32))
        l_t = jnp.moveaxis(l, -1, 1)
        return o / l_t[..., None], jnp.moveaxis(m, -1, 1) + jnp.log(l_t)

    o, lse = lax.map(one, (qp, kw, vw))
    o = o.reshape(N, Lp, Hk, G, Dv)[:, :L]
    lse = lse.reshape(N, Lp, Hk, G)[:, :L]
    return o, lse


def to_strided(x, r):
    B, S = x.shape[:2]
    rest = x.shape[2:]
    x = x.reshape((B, S // r, r) + rest)
    return jnp.moveaxis(x, 2, 1).reshape((B * r, S // r) + rest)


def from_strided(x, r, B):
    L = x.shape[1]
    rest = x.shape[2:]
    x = x.reshape((B, r, L) + rest)
    return jnp.moveaxis(x, 1, 2).reshape((B, L * r) + rest)


def short_conv_mixer(h, w_in, conv_k, w_out):
    b_gate, c_gate, xin, z = jnp.split(h @ w_in, 4, axis=-1)
    t = c_gate * xin
    E = t.shape[-1]
    pad = (CONV_WIDTH - 1) // 2
    conv = lax.conv_general_dilated(
        t, conv_k.astype(t.dtype)[:, None, :], window_strides=(1,), padding=[(pad, pad)],
        dimension_numbers=('NWC', 'WIO', 'NWC'), feature_group_count=E)
    return (b_gate * conv * jax.nn.silu(z)) @ w_out


def dilated_mixer(h, w_in, w_out, cos, sin):
    B, S, _ = h.shape
    G, H, Dh = len(DIL_RATES), DIL_HEADS, DIL_HEAD_DIM
    E = H * Dh
    q, k, v, z = jnp.split(h @ w_in, [G * E, 2 * G * E, 2 * G * E + E], axis=-1)
    q = apply_rope(q.reshape(B, S, G, H, Dh), cos, sin)
    k = apply_rope(k.reshape(B, S, G, H, Dh), cos, sin)
    v = v.reshape(B, S, H, Dh)
    outs, lses = [], []
    for g in range(G):
        r = DIL_RATES[g]
        half = DIL_WINDOWS[g] // (2 * r)
        o, lse = banded_attention(to_strided(q[:, :, g], r)[:, :, :, None],
                                  to_strided(k[:, :, g], r), to_strided(v, r), half)
        outs.append(from_strided(o[:, :, :, 0], r, B))
        lses.append(from_strided(lse[..., 0], r, B))
    wts = jax.nn.softmax(jnp.stack(lses, axis=0), axis=0)
    o = jnp.sum(wts[..., None] * jnp.stack(outs, axis=0), axis=0)
    y = o.reshape(B, S, E).astype(h.dtype) * jax.nn.silu(z)
    return y @ w_out


def window_gqa_mixer(h, w_in, sink, w_out, cos, sin):
    B, S, _ = h.shape
    Hq, Hk, Dh = SWA_Q_HEADS, SWA_KV_HEADS, SWA_HEAD_DIM
    G = Hq // Hk
    q, k, v, z = jnp.split(h @ w_in, [Hq * Dh, Hq * Dh + Hk * Dh, Hq * Dh + 2 * Hk * Dh], axis=-1)
    q = apply_rope(q.reshape(B, S, Hk, G, Dh), cos, sin)
    k = apply_rope(k.reshape(B, S, Hk, Dh), cos, sin)
    v = v.reshape(B, S, Hk, Dh)
    o, _ = banded_attention(q, k, v, SWA_HALF, sink.reshape(Hk, G))
    y = o.reshape(B, S, Hq * Dh).astype(h.dtype) * jax.nn.silu(z)
    return y @ w_out


def diff_attention_mixer(h, w_in, lam_vecs, subln_g, w_out, cos, sin, layer_idx):
    B, S, _ = h.shape
    H, d = DIFF_HEADS, DIFF_HEAD_DIM
    q, k, v, z = jnp.split(h @ w_in, 4, axis=-1)
    q = apply_rope(q.reshape(B, S, H, 2, d), cos, sin)
    k = apply_rope(k.reshape(B, S, H, 2, d), cos, sin)
    v = v.reshape(B, S, H, 2 * d).astype(jnp.float32)
    lam_init = 0.8 - 0.6 * math.exp(-0.3 * layer_idx)
    lv = lam_vecs.astype(jnp.float32)
    lam = jnp.exp(jnp.sum(lv[0] * lv[1])) - jnp.exp(jnp.sum(lv[2] * lv[3])) + lam_init
    nb = S // Q_BLOCK
    qb = jnp.moveaxis(q.reshape(B, nb, Q_BLOCK, H, 2, d), 1, 0)
    scale = d ** -0.5

    def block(qs):
        s = jnp.einsum('bqhcd,bkhcd->bhcqk', qs, k, preferred_element_type=jnp.float32) * scale
        p = jax.nn.softmax(s, axis=-1)
        a = p[:, :, 0] - lam * p[:, :, 1]
        return jnp.einsum('bhqk,bkhe->bqhe', a, v)

    o = jnp.moveaxis(lax.map(block, qb), 0, 1).reshape(B, S, H, 2 * d)
    o = rms_norm(o, subln_g, SUBLN_EPS) * (1.0 - lam_init)
    y = o.reshape(B, S, H * 2 * d).astype(h.dtype) * jax.nn.silu(z)
    return y @ w_out


def setup_inputs(seed: int = 0) -> dict:
    key = jax.random.key(seed)
    ks = iter(jax.random.split(key, 32))
    nrm = lambda shape, s: jax.random.normal(next(ks), shape, jnp.float32) * s
    D = D_MODEL
    nA, nB, nC, nD = (_layers_of_type(t) for t in range(N_MIXERS))
    G = len(DIL_RATES)
    swa_in = 2 * SWA_DIM + 2 * SWA_KV_HEADS * SWA_HEAD_DIM
    return dict(
        x=nrm((BATCH, SEQ, D), 1.0),
        c=nrm((BATCH, D), 1.0),
        norm_g=1.0 + nrm((DEPTH, D), 0.02),
        w_mod=nrm((DEPTH, D, 3 * D), 0.5 * D ** -0.5),
        b_mod=nrm((DEPTH, 3 * D), 0.02),
        conv_w_in=nrm((nA, D, 4 * CONV_DIM), D ** -0.5),
        conv_k=nrm((nA, CONV_WIDTH, CONV_DIM), CONV_WIDTH ** -0.5),
        conv_w_out=nrm((nA, CONV_DIM, D), CONV_DIM ** -0.5),
        dil_w_in=nrm((nB, D, (2 * G + 2) * DIL_DIM), D ** -0.5),
        dil_w_out=nrm((nB, DIL_DIM, D), DIL_DIM ** -0.5),
        swa_w_in=nrm((nC, D, swa_in), D ** -0.5),
        swa_sink=nrm((nC, SWA_Q_HEADS), 1.0),
        swa_w_out=nrm((nC, SWA_DIM, D), SWA_DIM ** -0.5),
        diff_w_in=nrm((nD, D, 4 * DIFF_DIM), D ** -0.5),
        diff_lambda=nrm((nD, 4, DIFF_HEAD_DIM), 0.1),
        diff_subln_g=1.0 + nrm((nD, 2 * DIFF_HEAD_DIM), 0.02),
        diff_w_out=nrm((nD, DIFF_DIM, D), DIFF_DIM ** -0.5),
        final_g=1.0 + nrm((D,), 0.02),
    )


def reference(x, c, norm_g, w_mod, b_mod, conv_w_in, conv_k, conv_w_out, dil_w_in, dil_w_out,
              swa_w_in, swa_sink, swa_w_out, diff_w_in, diff_lambda, diff_subln_g, diff_w_out,
              final_g):
    S = x.shape[1]
    cos_dil, sin_dil = rope_tables(S, DIL_HEAD_DIM)
    cos_swa, sin_swa = rope_tables(S, SWA_HEAD_DIM)
    cos_diff, sin_diff = rope_tables(S, DIFF_HEAD_DIM)
    for i in range(DEPTH):
        kind, j = i % N_MIXERS, i // N_MIXERS
        h, gate = adaln_pre_norm(x, c, norm_g[i], w_mod[i], b_mod[i])
        if kind == 0:
            y = short_conv_mixer(h, conv_w_in[j], conv_k[j], conv_w_out[j])
        elif kind == 1:
            y = dilated_mixer(h, dil_w_in[j], dil_w_out[j], cos_dil, sin_dil)
        elif kind == 2:
            y = window_gqa_mixer(h, swa_w_in[j], swa_sink[j], swa_w_out[j], cos_swa, sin_swa)
        else:
            y = diff_attention_mixer(h, diff_w_in[j], diff_lambda[j], diff_subln_g[j],
                                     diff_w_out[j], cos_diff, sin_diff, i)
        x = x + gate[:, None, :] * y
    return rms_norm(x, final_g, NORM_EPS)
```

```python
import functools
import math

import numpy as np
import jax
import jax.numpy as jnp
from jax import lax
from jax.experimental import pallas as pl
from jax.experimental.pallas import tpu as pltpu

F32 = jnp.float32
BF16 = jnp.bfloat16

LANES = 128
VMEM_LIMIT = 56 * 1024 * 1024

N_MIXERS = 4
CONV_WIDTH = 3
DIL_WINDOWS = (128, 512, 2048)
DIL_RATES = (1, 4, 16)
DIL_HEADS = 8
SWA_HALF = 128
SWA_Q_HEADS = 16
SWA_KV_HEADS = 4
SWA_HEAD_DIM = 64
DIFF_HEADS = 8
DIFF_HEAD_DIM = 64
ROPE_THETA = 10000.0
NORM_EPS = 1e-6
SUBLN_EPS = 1e-5
NEG_INF = -1e30


def _cparams(sem):
    return pltpu.CompilerParams(dimension_semantics=sem, vmem_limit_bytes=VMEM_LIMIT)


def _resident(shape):
    nd = len(shape)
    return pl.BlockSpec(shape, lambda *_: (0,) * nd)


def _mod_body(c_ref, w_ref, b_ref, o_ref):
    c = c_ref[...]
    sc = c * jax.nn.sigmoid(c)
    o_ref[0, 0] = jnp.dot(sc, w_ref[0], precision=lax.Precision.HIGHEST,
                          preferred_element_type=F32) + b_ref[0, 0]


def _modulation(c, w_mod, b_mod):
    depth, d, _ = w_mod.shape
    b = c.shape[0]
    rows = -(-b // 8) * 8
    cp = jnp.pad(c, ((0, rows - b), (0, 0)))
    out = pl.pallas_call(
        _mod_body,
        grid=(depth, 3),
        in_specs=[
            pl.BlockSpec((rows, d), lambda i, j: (0, 0)),
            pl.BlockSpec((1, d, d), lambda i, j: (i, 0, j)),
            pl.BlockSpec((1, 1, 1, d), lambda i, j: (i, j, 0, 0)),
        ],
        out_specs=pl.BlockSpec((1, 1, rows, d), lambda i, j: (i, j, 0, 0)),
        out_shape=jax.ShapeDtypeStruct((depth, 3, rows, d), F32),
        compiler_params=_cparams(("arbitrary", "arbitrary")),
    )(cp, w_mod, b_mod.reshape(depth, 3, 1, d))
    return out[:, :, :b]


def _silu(a):
    return a * jax.nn.sigmoid(a)


def _inproj_body(ops, n_tab, n_out, chunk, x_ref, shift_ref, scale_ref, g_ref, w_ref, *rest):
    tabs = rest[:n_tab]
    outs = rest[n_tab:n_tab + n_out]
    h_ref = rest[n_tab + n_out]
    x = x_ref[0]
    ms = jnp.mean(x * x, axis=-1, keepdims=True)
    y = x * lax.rsqrt(ms + NORM_EPS) * g_ref[...]
    h_ref[...] = (y * (1.0 + scale_ref[0]) + shift_ref[0]).astype(BF16)

    def proj(c0, n):
        return jnp.dot(h_ref[...], w_ref[:, c0:c0 + n], preferred_element_type=F32)

    for kind, srcs, ncols, oi, ooff, tab in ops:
        o_ref = outs[oi]
        for c in range(0, ncols, chunk):
            n = min(chunk, ncols - c)
            if kind == "plain":
                r = proj(srcs[0] + c, n)
            elif kind == "silu":
                r = _silu(proj(srcs[0] + c, n))
            elif kind == "mul":
                r = proj(srcs[0] + c, n) * proj(srcs[1] + c, n)
            elif kind == "mulsilu":
                r = proj(srcs[0] + c, n) * _silu(proj(srcs[1] + c, n))
            elif kind == "rope":
                a = proj(srcs[0] + c, n)
                cos = tabs[tab[0]][...]
                sin = tabs[tab[1]][...]
                parts = []
                for j in range(n // LANES):
                    aj = a[:, j * LANES:(j + 1) * LANES]
                    parts.append(aj * cos + pltpu.roll(aj, LANES // 2, 1) * sin)
                r = parts[0] if len(parts) == 1 else jnp.concatenate(parts, axis=1)
            else:
                raise ValueError(kind)
            o_ref[0, :, ooff + c:ooff + c + n] = r.astype(o_ref.dtype)


def _inproj(x, shift, scale, g, w, tables, out_widths, ops, tm, chunk=512):
    b, s, d = x.shape
    n = w.shape[1]
    in_specs = [
        pl.BlockSpec((1, tm, d), lambda bi, i: (bi, i, 0)),
        pl.BlockSpec((1, 1, d), lambda bi, i: (bi, 0, 0)),
        pl.BlockSpec((1, 1, d), lambda bi, i: (bi, 0, 0)),
        _resident((1, d)),
        pl.BlockSpec((d, n), lambda bi, i: (0, 0), pipeline_mode=pl.Buffered(1)),
    ] + [pl.BlockSpec((tm, LANES), lambda bi, i: (i, 0)) for _ in tables]
    out_specs = [pl.BlockSpec((1, tm, wd), lambda bi, i: (bi, i, 0)) for wd in out_widths]
    out_shape = [jax.ShapeDtypeStruct((b, s, wd), BF16) for wd in out_widths]
    body = functools.partial(_inproj_body, tuple(ops), len(tables), len(out_widths), chunk)
    return pl.pallas_call(
        body,
        grid=(b, s // tm),
        in_specs=in_specs,
        out_specs=out_specs,
        out_shape=out_shape,
        scratch_shapes=[pltpu.VMEM((tm, d), BF16)],
        compiler_params=_cparams(("parallel", "parallel")),
    )(x, shift, scale, g, w, *tables)


def _outproj_tail(y, w_ref, x_ref, gate_ref, fg_ref, o_ref):
    acc = jnp.dot(y.astype(BF16), w_ref[...], preferred_element_type=F32)
    xn = x_ref[0] + gate_ref[0] * acc
    if fg_ref is not None:
        ms = jnp.mean(xn * xn, axis=-1, keepdims=True)
        xn = xn * lax.rsqrt(ms + NORM_EPS) * fg_ref[...]
    o_ref[0] = xn


def _outproj_conv_body(halo, t_prev_ref, t_ref, t_next_ref, g_ref, ck_ref, w_ref, x_ref,
                       gate_ref, o_ref, win_ref):
    i = pl.program_id(1)
    tm = t_ref.shape[1]
    has_prev = (i > 0).astype(F32)
    has_next = (i < pl.num_programs(1) - 1).astype(F32)
    win_ref[0:halo] = t_prev_ref[0].astype(F32) * has_prev
    win_ref[halo:halo + tm] = t_ref[0].astype(F32)
    win_ref[halo + tm:2 * halo + tm] = t_next_ref[0].astype(F32) * has_next
    ck = ck_ref[...]
    conv = (win_ref[halo - 1:halo - 1 + tm] * ck[0:1]
            + win_ref[halo:halo + tm] * ck[1:2]
            + win_ref[halo + 1:halo + 1 + tm] * ck[2:3])
    y = g_ref[0].astype(F32) * conv
    _outproj_tail(y, w_ref, x_ref, gate_ref, None, o_ref)


def _outproj_dil_body(n_grp, *refs):
    o_refs = refs[:n_grp]
    lse_refs = refs[n_grp:2 * n_grp]
    zs_ref, expand_ref, w_ref, x_ref, gate_ref, out_ref = refs[2 * n_grp:]
    lses = [r[0] for r in lse_refs]
    mx = functools.reduce(jnp.maximum, lses)
    es = [jnp.exp(l - mx) for l in lses]
    inv = 1.0 / functools.reduce(jnp.add, es)
    o = None
    for gi in range(n_grp):
        wt = es[gi] * inv
        hi = wt.astype(BF16)
        lo = (wt - hi.astype(F32)).astype(BF16)
        wfull = jnp.dot(jnp.concatenate([hi, lo], axis=1), expand_ref[...],
                        preferred_element_type=F32)
        term = wfull * o_refs[gi][0].astype(F32)
        o = term if o is None else o + term
    y = o * zs_ref[0].astype(F32)
    _outproj_tail(y, w_ref, x_ref, gate_ref, None, out_ref)


def _outproj_mul_body(final, o_ref_in, zs_ref, w_ref, x_ref, gate_ref, *rest):
    fg_ref = rest[0] if final else None
    out_ref = rest[-1]
    y = o_ref_in[0].astype(F32) * zs_ref[0].astype(F32)
    _outproj_tail(y, w_ref, x_ref, gate_ref, fg_ref, out_ref)


def _tile_spec(tm, d):
    return pl.BlockSpec((1, tm, d), lambda bi, i: (bi, i, 0))


def _outproj_common_specs(tm, d):
    return [
        _resident((d, d)),
        _tile_spec(tm, d),
        pl.BlockSpec((1, 1, d), lambda bi, i: (bi, 0, 0)),
    ]


def _outproj_call(body, in_specs, args, x, tm, scratch=()):
    b, s, d = x.shape
    return pl.pallas_call(
        body,
        grid=(b, s // tm),
        in_specs=in_specs,
        out_specs=_tile_spec(tm, d),
        out_shape=jax.ShapeDtypeStruct((b, s, d), F32),
        scratch_shapes=list(scratch),
        compiler_params=_cparams(("parallel", "parallel")),
    )(*args)


def _fill_window(win_ref, prev_ref, cur_ref, next_ref, half, tq_blk):
    win_ref[0:half] = prev_ref[0]
    win_ref[half:half + tq_blk] = cur_ref[0]
    win_ref[half + tq_blk:2 * half + tq_blk] = next_ref[0]


def _band_mask(half, tq, j, n_sub, blk, n_blk):
    nk = tq + 2 * half
    row = lax.broadcasted_iota(jnp.int32, (tq, nk), 0)
    col = lax.broadcasted_iota(jnp.int32, (tq, nk), 1)
    delta = col - row
    mask = (delta >= 0) & (delta <= 2 * half)
    if j == 0:
        mask = mask & ((col >= half) | (blk > 0))
    if j == n_sub - 1:
        mask = mask & ((col < tq + half) | (blk < n_blk - 1))
    return mask


def _softmax_pv(s, mask, v, sink=None):
    s = jnp.where(mask, s, NEG_INF)
    m = jnp.max(s, axis=-1, keepdims=True)
    if sink is not None:
        m = jnp.maximum(m, sink)
    p = jnp.exp(s - m)
    l = jnp.sum(p, axis=-1, keepdims=True)
    if sink is not None:
        l = l + jnp.exp(sink - m)
    o = jnp.dot(p.astype(BF16), v, preferred_element_type=F32)
    return o / l, m, l


def _qk(q, k):
    return lax.dot_general(q, k, (((1,), (1,)), ((), ())), preferred_element_type=F32)


def _dil_attn_body(half, n_heads, q_ref, kp_ref, kc_ref, kn_ref, vp_ref, vc_ref, vn_ref,
                   o_ref, lse_ref, kwin_ref, vwin_ref):
    blk = pl.program_id(2)
    n_blk = pl.num_programs(2)
    tq_blk = q_ref.shape[1]
    tq = 2 * half
    n_sub = tq_blk // tq
    _fill_window(kwin_ref, kp_ref, kc_ref, kn_ref, half, tq_blk)
    _fill_window(vwin_ref, vp_ref, vc_ref, vn_ref, half, tq_blk)
    lane = lax.broadcasted_iota(jnp.int32, (tq, LANES), 1)
    for j in range(n_sub):
        mask = _band_mask(half, tq, j, n_sub, blk, n_blk)
        r0 = j * tq
        lse_tile = jnp.zeros((tq, LANES), F32)
        for h in range(n_heads):
            cs = slice(h * LANES, (h + 1) * LANES)
            s = _qk(q_ref[0, r0:r0 + tq, cs], kwin_ref[r0:r0 + tq + 2 * half, cs])
            o, m, l = _softmax_pv(s, mask, vwin_ref[r0:r0 + tq + 2 * half, cs])
            o_ref[0, r0:r0 + tq, cs] = o.astype(o_ref.dtype)
            lse_tile = jnp.where(lane == h, m + jnp.log(l), lse_tile)
        lse_ref[0, r0:r0 + tq, :] = lse_tile


def _dil_attention(q_all, k_all, v, grp, n_grp, rate, half, tq_blk):
    b, s, c = v.shape
    l = s // rate
    tq_blk = min(tq_blk, l)
    n_blk = l // tq_blk
    hb = tq_blk // half
    last_halo = l // half - 1
    view = lambda t: t.reshape(b, l, rate * t.shape[-1])

    def specs(col):
        cur = pl.BlockSpec((1, tq_blk, c), lambda bi, p, i: (bi, i, col(p)))
        prev = pl.BlockSpec((1, half, c), lambda bi, p, i: (bi, jnp.maximum(i * hb - 1, 0), col(p)))
        nxt = pl.BlockSpec((1, half, c),
                           lambda bi, p, i: (bi, jnp.minimum((i + 1) * hb, last_halo), col(p)))
        return cur, prev, nxt

    qc, qp, qn = specs(lambda p: p * n_grp + grp)
    vc, vp, vn = specs(lambda p: p)
    body = functools.partial(_dil_attn_body, half, c // LANES)
    o, lse = pl.pallas_call(
        body,
        grid=(b, rate, n_blk),
        in_specs=[qc, qp, qc, qn, vp, vc, vn],
        out_specs=[vc, pl.BlockSpec((1, tq_blk, LANES), lambda bi, p, i: (bi, i, p))],
        out_shape=[jax.ShapeDtypeStruct((b, l, rate * c), BF16),
                   jax.ShapeDtypeStruct((b, l, rate * LANES), F32)],
        scratch_shapes=[pltpu.VMEM((tq_blk + 2 * half, c), BF16),
                        pltpu.VMEM((tq_blk + 2 * half, c), BF16)],
        compiler_params=_cparams(("parallel", "parallel", "parallel")),
    )(view(q_all), view(k_all), view(k_all), view(k_all), view(v), view(v), view(v))
    return o.reshape(b, s, c), lse.reshape(b, s, LANES)


def _swa_attn_body(half, n_qv, q_per_kv, sink_ref, q_ref, kp_ref, kc_ref, kn_ref,
                   vp_ref, vc_ref, vn_ref, o_ref, kwin_ref, vwin_ref):
    blk = pl.program_id(1)
    n_blk = pl.num_programs(1)
    tq_blk = q_ref.shape[1]
    tq = 2 * half
    n_sub = tq_blk // tq
    _fill_window(kwin_ref, kp_ref, kc_ref, kn_ref, half, tq_blk)
    _fill_window(vwin_ref, vp_ref, vc_ref, vn_ref, half, tq_blk)
    lane = lax.broadcasted_iota(jnp.int32, (tq, LANES), 1)
    first = (lane % (LANES // 2)) < (LANES // 4)
    for j in range(n_sub):
        mask = _band_mask(half, tq, j, n_sub, blk, n_blk)
        r0 = j * tq
        for u in range(n_qv):
            kv = u // q_per_kv
            cs = slice(u * LANES, (u + 1) * LANES)
            ks = slice(kv * LANES, (kv + 1) * LANES)
            q = q_ref[0, r0:r0 + tq, cs]
            kw = kwin_ref[r0:r0 + tq + 2 * half, ks]
            vw = vwin_ref[r0:r0 + tq + 2 * half, ks]
            zero = jnp.zeros_like(q)
            oa, _, _ = _softmax_pv(_qk(jnp.where(first, q, zero), kw), mask, vw, sink_ref[2 * u])
            ob, _, _ = _softmax_pv(_qk(jnp.where(first, zero, q), kw), mask, vw, sink_ref[2 * u + 1])
            o_ref[0, r0:r0 + tq, cs] = jnp.where(lane < LANES // 2, oa, ob).astype(o_ref.dtype)


def _swa_attention(q, k, v, sink_pairs, half, tq_blk, q_per_kv):
    b, s, cq = q.shape
    ck = k.shape[-1]
    tq_blk = min(tq_blk, s)
    n_blk = s // tq_blk
    hb = tq_blk // half
    last_halo = s // half - 1
    qspec = pl.BlockSpec((1, tq_blk, cq), lambda bi, i: (bi, i, 0))
    cur = pl.BlockSpec((1, tq_blk, ck), lambda bi, i: (bi, i, 0))
    prev = pl.BlockSpec((1, half, ck), lambda bi, i: (bi, jnp.maximum(i * hb - 1, 0), 0))
    nxt = pl.BlockSpec((1, half, ck), lambda bi, i: (bi, jnp.minimum((i + 1) * hb, last_halo), 0))
    body = functools.partial(_swa_attn_body, half, cq // LANES, q_per_kv)
    return pl.pallas_call(
        body,
        grid=(b, n_blk),
        in_specs=[pl.BlockSpec(memory_space=pltpu.SMEM), qspec, prev, cur, nxt, prev, cur, nxt],
        out_specs=qspec,
        out_shape=jax.ShapeDtypeStruct((b, s, cq), BF16),
        scratch_shapes=[pltpu.VMEM((tq_blk + 2 * half, ck), BF16),
                        pltpu.VMEM((tq_blk + 2 * half, ck), BF16)],
        compiler_params=_cparams(("parallel", "parallel")),
    )(sink_pairs, q, k, k, k, v, v, v)


def _diff_attn_body(tk, lam_init, lam_ref, sg_ref, q_ref, k_ref, v_ref, o_ref,
                    qm_ref, m_ref, l_ref, acc_ref):
    tq = q_ref.shape[1]
    n_kv = k_ref.shape[1] // tk
    lane = lax.broadcasted_iota(jnp.int32, (tq, LANES), 1)
    first = (lane % (LANES // 2)) < (LANES // 4)
    q = q_ref[0]
    zero = jnp.zeros_like(q)
    qm_ref[0:tq] = jnp.where(first, q, zero)
    qm_ref[tq:2 * tq] = jnp.where(first, zero, q)
    m_ref[...] = jnp.full(m_ref.shape, NEG_INF, F32)
    l_ref[...] = jnp.zeros(l_ref.shape, F32)
    acc_ref[...] = jnp.zeros(acc_ref.shape, F32)

    def step(kk, carry):
        k0 = pl.multiple_of(kk * tk, tk)
        s = _qk(qm_ref[...], k_ref[0, pl.ds(k0, tk), :])
        m_old = m_ref[...]
        m_new = jnp.maximum(m_old, jnp.max(s, axis=-1, keepdims=True))
        alpha = jnp.exp(m_old - m_new)
        p = jnp.exp(s - m_new)
        l_ref[...] = alpha * l_ref[...] + jnp.sum(p, axis=-1, keepdims=True)
        acc_ref[...] = alpha * acc_ref[...] + jnp.dot(
            p.astype(BF16), v_ref[0, pl.ds(k0, tk), :], preferred_element_type=F32)
        m_ref[...] = m_new
        return carry

    lax.fori_loop(0, n_kv, step, 0)

    lv = lam_ref[...]
    lam = (jnp.exp(jnp.sum(lv[0:1] * lv[1:2], axis=-1, keepdims=True))
           - jnp.exp(jnp.sum(lv[2:3] * lv[3:4], axis=-1, keepdims=True)) + lam_init)
    o_all = acc_ref[...] / l_ref[...]
    o = o_all[0:tq] - lam * o_all[tq:2 * tq]
    ms = jnp.mean(o * o, axis=-1, keepdims=True)
    o = o * lax.rsqrt(ms + SUBLN_EPS) * sg_ref[...] * (1.0 - lam_init)
    o_ref[0] = o.astype(o_ref.dtype)


def _diff_attention(q, k, v, lam_vecs, subln_g, lam_init, tq, tk):
    b, s, c = q.shape
    tq = min(tq, s)
    tk = min(tk, s)
    n_heads = c // LANES
    body = functools.partial(_diff_attn_body, tk, lam_init)
    qspec = pl.BlockSpec((1, tq, LANES), lambda bi, h, i: (bi, i, h))
    kvspec = pl.BlockSpec((1, s, LANES), lambda bi, h, i: (bi, 0, h))
    return pl.pallas_call(
        body,
        grid=(b, n_heads, s // tq),
        in_specs=[_resident(lam_vecs.shape), _resident((1, LANES)), qspec, kvspec, kvspec],
        out_specs=qspec,
        out_shape=jax.ShapeDtypeStruct((b, s, c), BF16),
        scratch_shapes=[pltpu.VMEM((2 * tq, LANES), BF16),
                        pltpu.VMEM((2 * tq, 1), F32),
                        pltpu.VMEM((2 * tq, 1), F32),
                        pltpu.VMEM((2 * tq, LANES), F32)],
        compiler_params=_cparams(("parallel", "parallel", "arbitrary")),
    )(lam_vecs, subln_g.reshape(1, LANES), q, k, v)


def _rope_tables(seq, dim, scale):
    inv = ROPE_THETA ** (-jnp.arange(0, dim, 2, dtype=F32) / dim)
    ang = jnp.arange(seq, dtype=F32)[:, None] * inv[None, :]
    reps = (LANES // 2) // (dim // 2)
    cos = jnp.tile(jnp.cos(ang), (1, 2 * reps))
    sin = jnp.tile(jnp.sin(ang), (1, reps))
    sin = jnp.concatenate([-sin, sin], axis=1)
    return cos, sin, cos * scale, sin * scale


def _pair_interleave(a0, b0, half_dim):
    r = np.arange(half_dim)
    return np.concatenate([a0 + r, b0 + r, a0 + half_dim + r, b0 + half_dim + r])


def _cast_w(w, cols=None):
    if cols is not None:
        w = w[:, cols]
    return w.astype(BF16)


def _conv_layer(x, shift, scale, gate, g, w_in, conv_k, w_out, tm):
    b, s, d = x.shape
    e = w_out.shape[0]
    ops = [("mul", (e, 2 * e), e, 0, 0, None),
           ("mulsilu", (0, 3 * e), e, 1, 0, None)]
    t, gz = _inproj(x, shift, scale, g, _cast_w(w_in), [], [e, e], ops, tm)
    halo = 16
    hb = tm // halo
    last = s // halo - 1
    specs = [
        pl.BlockSpec((1, halo, e), lambda bi, i: (bi, jnp.maximum(i * hb - 1, 0), 0)),
        _tile_spec(tm, e),
        pl.BlockSpec((1, halo, e), lambda bi, i: (bi, jnp.minimum((i + 1) * hb, last), 0)),
        _tile_spec(tm, e),
        _resident(conv_k.shape),
    ] + _outproj_common_specs(tm, d)
    body = functools.partial(_outproj_conv_body, halo)
    return _outproj_call(body, specs, (t, t, t, gz, conv_k, _cast_w(w_out), x, gate), x, tm,
                         scratch=[pltpu.VMEM((tm + 2 * halo, e), F32)])


def _dil_layer(x, shift, scale, gate, g, w_in, w_out, tm):
    b, s, d = x.shape
    n_grp = len(DIL_RATES)
    e = w_out.shape[0]
    dh = e // DIL_HEADS
    cos, sin, cos_q, sin_q = _rope_tables(s, dh, dh ** -0.5)
    ops = [("rope", (0,), n_grp * e, 0, 0, (2, 3)),
           ("rope", (n_grp * e,), n_grp * e, 1, 0, (0, 1)),
           ("plain", (2 * n_grp * e,), e, 2, 0, None),
           ("silu", (2 * n_grp * e + e,), e, 3, 0, None)]
    q, k, v, zs = _inproj(x, shift, scale, g, _cast_w(w_in), [cos, sin, cos_q, sin_q],
                          [n_grp * e, n_grp * e, e, e], ops, tm)
    os_, lses = [], []
    for gi, rate in enumerate(DIL_RATES):
        half = DIL_WINDOWS[gi] // (2 * rate)
        o, lse = _dil_attention(q, k, v, gi, n_grp, rate, half, 512)
        os_.append(o)
        lses.append(lse)
    rows = np.arange(2 * LANES)[:, None] % LANES
    expand = jnp.asarray(rows == (np.arange(e)[None, :] // dh), dtype=BF16)
    specs = ([_tile_spec(tm, e)] * n_grp + [_tile_spec(tm, LANES)] * n_grp
             + [_tile_spec(tm, e), _resident(expand.shape)] + _outproj_common_specs(tm, d))
    body = functools.partial(_outproj_dil_body, n_grp)
    return _outproj_call(body, specs, (*os_, *lses, zs, expand, _cast_w(w_out), x, gate), x, tm)


def _swa_layer(x, shift, scale, gate, g, w_in, sink, w_out, tm):
    b, s, d = x.shape
    hq, hk, dh = SWA_Q_HEADS, SWA_KV_HEADS, SWA_HEAD_DIM
    grp = hq // hk
    pairs = [((2 * p) * grp + gi, (2 * p + 1) * grp + gi) for p in range(hk // 2) for gi in range(grp)]
    q_cols = np.concatenate([_pair_interleave(a * dh, b_ * dh, dh // 2) for a, b_ in pairs])
    k_cols = hq * dh + np.concatenate(
        [_pair_interleave(2 * p * dh, (2 * p + 1) * dh, dh // 2) for p in range(hk // 2)])
    v_cols = hq * dh + hk * dh + np.arange(hk * dh)
    head_cols = np.concatenate([np.concatenate([a * dh + np.arange(dh), b_ * dh + np.arange(dh)])
                                for a, b_ in pairs])
    z_cols = hq * dh + 2 * hk * dh + head_cols
    w = _cast_w(w_in, np.concatenate([q_cols, k_cols, v_cols, z_cols]))
    sink_pairs = sink[np.array([h for pr in pairs for h in pr])]
    cos, sin, cos_q, sin_q = _rope_tables(s, dh, dh ** -0.5)
    nq, nk = hq * dh, hk * dh
    ops = [("rope", (0,), nq, 0, 0, (2, 3)),
           ("rope", (nq,), nk, 1, 0, (0, 1)),
           ("plain", (nq + nk,), nk, 2, 0, None),
           ("silu", (nq + 2 * nk,), nq, 3, 0, None)]
    q, k, v, zs = _inproj(x, shift, scale, g, w, [cos, sin, cos_q, sin_q], [nq, nk, nk, nq], ops, tm)
    o = _swa_attention(q, k, v, sink_pairs, SWA_HALF, 512, grp)
    specs = [_tile_spec(tm, nq), _tile_spec(tm, nq)] + _outproj_common_specs(tm, d)
    body = functools.partial(_outproj_mul_body, False)
    return _outproj_call(body, specs, (o, zs, _cast_w(w_out[head_cols]), x, gate), x, tm)


def _diff_layer(x, shift, scale, gate, g, w_in, lam_vecs, subln_g, w_out, final_g, layer_idx, tm):
    b, s, d = x.shape
    nh, dh = DIFF_HEADS, DIFF_HEAD_DIM
    e = nh * 2 * dh
    qk_cols = np.concatenate([_pair_interleave(h * 2 * dh, h * 2 * dh + dh, dh // 2) for h in range(nh)])
    cols = np.concatenate([qk_cols, e + qk_cols, 2 * e + np.arange(2 * e)])
    w = _cast_w(w_in, cols)
    cos, sin, cos_q, sin_q = _rope_tables(s, dh, dh ** -0.5)
    ops = [("rope", (0,), e, 0, 0, (2, 3)),
           ("rope", (e,), e, 1, 0, (0, 1)),
           ("plain", (2 * e,), e, 2, 0, None),
           ("silu", (3 * e,), e, 3, 0, None)]
    q, k, v, zs = _inproj(x, shift, scale, g, w, [cos, sin, cos_q, sin_q], [e, e, e, e], ops, tm)
    lam_init = 0.8 - 0.6 * math.exp(-0.3 * layer_idx)
    o = _diff_attention(q, k, v, lam_vecs, subln_g, lam_init, 512, 512)
    specs = ([_tile_spec(tm, e), _tile_spec(tm, e)] + _outproj_common_specs(tm, d)
             + ([_resident((1, d))] if final_g is not None else []))
    body = functools.partial(_outproj_mul_body, final_g is not None)
    args = (o, zs, _cast_w(w_out), x, gate) + ((final_g.reshape(1, d),) if final_g is not None else ())
    return _outproj_call(body, specs, args, x, tm)


def kernel(x, c, norm_g, w_mod, b_mod, conv_w_in, conv_k, conv_w_out, dil_w_in, dil_w_out,
           swa_w_in, swa_sink, swa_w_out, diff_w_in, diff_lambda, diff_subln_g, diff_w_out,
           final_g):
    b, s, d = x.shape
    depth = norm_g.shape[0]
    tm = min(512, s)
    mod = _modulation(c, w_mod, b_mod)
    for i in range(depth):
        kind, j = i % N_MIXERS, i // N_MIXERS
        shift, scale, gate = (mod[i, t][:, None, :] for t in range(3))
        g = norm_g[i].reshape(1, d)
        last = i == depth - 1
        if kind == 0:
            x = _conv_layer(x, shift, scale, gate, g, conv_w_in[j], conv_k[j], conv_w_out[j], tm)
        elif kind == 1:
            x = _dil_layer(x, shift, scale, gate, g, dil_w_in[j], dil_w_out[j], tm)
        elif kind == 2:
            x = _swa_layer(x, shift, scale, gate, g, swa_w_in[j], swa_sink[j], swa_w_out[j], tm)
        else:
            x = _diff_layer(x, shift, scale, gate, g, diff_w_in[j], diff_lambda[j], diff_subln_g[j],
                            diff_w_out[j], final_g if last else None, i, tm)
        if last and kind != 3:
            raise NotImplementedError("final RMSNorm is fused into the differential layer")
    return x
```

```python
import functools
import math

import numpy as np
import jax
import jax.numpy as jnp
from jax import lax
from jax.experimental import pallas as pl
from jax.experimental.pallas import tpu as pltpu

F32 = jnp.float32
BF16 = jnp.bfloat16

LANES = 128
VMEM_LIMIT = 56 * 1024 * 1024

N_MIXERS = 4
CONV_WIDTH = 3
DIL_WINDOWS = (128, 512, 2048)
DIL_RATES = (1, 4, 16)
DIL_HEADS = 8
SWA_HALF = 128
SWA_Q_HEADS = 16
SWA_KV_HEADS = 4
SWA_HEAD_DIM = 64
DIFF_HEADS = 8
DIFF_HEAD_DIM = 64
ROPE_THETA = 10000.0
NORM_EPS = 1e-6
SUBLN_EPS = 1e-5
NEG_INF = -1e30


def _cparams(sem):
    return pltpu.CompilerParams(dimension_semantics=sem, vmem_limit_bytes=VMEM_LIMIT)


def _resident(shape):
    nd = len(shape)
    return pl.BlockSpec(shape, lambda *_: (0,) * nd)


def _mod_body(c_ref, w_ref, b_ref, o_ref):
    c = c_ref[...]
    sc = c * jax.nn.sigmoid(c)
    o_ref[0, 0] = jnp.dot(sc, w_ref[0], precision=lax.Precision.HIGHEST,
                          preferred_element_type=F32) + b_ref[0, 0]


def _modulation(c, w_mod, b_mod):
    depth, d, _ = w_mod.shape
    b = c.shape[0]
    rows = -(-b // 8) * 8
    cp = jnp.pad(c, ((0, rows - b), (0, 0)))
    out = pl.pallas_call(
        _mod_body,
        grid=(depth, 3),
        in_specs=[
            pl.BlockSpec((rows, d), lambda i, j: (0, 0)),
            pl.BlockSpec((1, d, d), lambda i, j: (i, 0, j)),
            pl.BlockSpec((1, 1, 1, d), lambda i, j: (i, j, 0, 0)),
        ],
        out_specs=pl.BlockSpec((1, 1, rows, d), lambda i, j: (i, j, 0, 0)),
        out_shape=jax.ShapeDtypeStruct((depth, 3, rows, d), F32),
        compiler_params=_cparams(("arbitrary", "arbitrary")),
    )(cp, w_mod, b_mod.reshape(depth, 3, 1, d))
    return out[:, :, :b]


def _silu(a):
    return a * jax.nn.sigmoid(a)


def _inproj_body(ops, n_tab, n_out, chunk, x_ref, shift_ref, scale_ref, g_ref, w_ref, *rest):
    tabs = rest[:n_tab]
    outs = rest[n_tab:n_tab + n_out]
    h_ref = rest[n_tab + n_out]
    x = x_ref[0]
    ms = jnp.mean(x * x, axis=-1, keepdims=True)
    y = x * lax.rsqrt(ms + NORM_EPS) * g_ref[...]
    h_ref[...] = (y * (1.0 + scale_ref[0]) + shift_ref[0]).astype(BF16)

    def proj(c0, n):
        return jnp.dot(h_ref[...], w_ref[:, c0:c0 + n], preferred_element_type=F32)

    for kind, srcs, ncols, oi, ooff, tab in ops:
        o_ref = outs[oi]
        for c in range(0, ncols, chunk):
            n = min(chunk, ncols - c)
            if kind == "plain":
                r = proj(srcs[0] + c, n)
            elif kind == "silu":
                r = _silu(proj(srcs[0] + c, n))
            elif kind == "mul":
                r = proj(srcs[0] + c, n) * proj(srcs[1] + c, n)
            elif kind == "mulsilu":
                r = proj(srcs[0] + c, n) * _silu(proj(srcs[1] + c, n))
            elif kind == "rope":
                a = proj(srcs[0] + c, n)
                cos = tabs[tab[0]][...]
                sin = tabs[tab[1]][...]
                parts = []
                for j in range(n // LANES):
                    aj = a[:, j * LANES:(j + 1) * LANES]
                    parts.append(aj * cos + pltpu.roll(aj, LANES // 2, 1) * sin)
                r = parts[0] if len(parts) == 1 else jnp.concatenate(parts, axis=1)
            else:
                raise ValueError(kind)
            o_ref[0, :, ooff + c:ooff + c + n] = r.astype(o_ref.dtype)


def _inproj(x, shift, scale, g, w, tables, out_widths, ops, tm, chunk=512):
    b, s, d = x.shape
    n = w.shape[1]
    in_specs = [
        pl.BlockSpec((1, tm, d), lambda bi, i: (bi, i, 0)),
        pl.BlockSpec((1, 1, d), lambda bi, i: (bi, 0, 0)),
        pl.BlockSpec((1, 1, d), lambda bi, i: (bi, 0, 0)),
        _resident((1, d)),
        pl.BlockSpec((d, n), lambda bi, i: (0, 0), pipeline_mode=pl.Buffered(1)),
    ] + [pl.BlockSpec((tm, LANES), lambda bi, i: (i, 0)) for _ in tables]
    out_specs = [pl.BlockSpec((1, tm, wd), lambda bi, i: (bi, i, 0)) for wd in out_widths]
    out_shape = [jax.ShapeDtypeStruct((b, s, wd), BF16) for wd in out_widths]
    body = functools.partial(_inproj_body, tuple(ops), len(tables), len(out_widths), chunk)
    return pl.pallas_call(
        body,
        grid=(b, s // tm),
        in_specs=in_specs,
        out_specs=out_specs,
        out_shape=out_shape,
        scratch_shapes=[pltpu.VMEM((tm, d), BF16)],
        compiler_params=_cparams(("parallel", "parallel")),
    )(x, shift, scale, g, w, *tables)


def _outproj_tail(y, w_ref, x_ref, gate_ref, fg_ref, o_ref):
    acc = jnp.dot(y.astype(BF16), w_ref[...], preferred_element_type=F32)
    xn = x_ref[0] + gate_ref[0] * acc
    if fg_ref is not None:
        ms = jnp.mean(xn * xn, axis=-1, keepdims=True)
        xn = xn * lax.rsqrt(ms + NORM_EPS) * fg_ref[...]
    o_ref[0] = xn


def _outproj_conv_body(halo, t_prev_ref, t_ref, t_next_ref, g_ref, ck_ref, w_ref, x_ref,
                       gate_ref, o_ref, win_ref):
    i = pl.program_id(1)
    tm = t_ref.shape[1]
    has_prev = (i > 0).astype(F32)
    has_next = (i < pl.num_programs(1) - 1).astype(F32)
    win_ref[0:halo] = t_prev_ref[0].astype(F32) * has_prev
    win_ref[halo:halo + tm] = t_ref[0].astype(F32)
    win_ref[halo + tm:2 * halo + tm] = t_next_ref[0].astype(F32) * has_next
    ck = ck_ref[...]
    conv = (win_ref[halo - 1:halo - 1 + tm] * ck[0:1]
            + win_ref[halo:halo + tm] * ck[1:2]
            + win_ref[halo + 1:halo + 1 + tm] * ck[2:3])
    y = g_ref[0].astype(F32) * conv
    _outproj_tail(y, w_ref, x_ref, gate_ref, None, o_ref)


def _outproj_dil_body(n_grp, *refs):
    o_refs = refs[:n_grp]
    lse_refs = refs[n_grp:2 * n_grp]
    zs_ref, expand_ref, w_ref, x_ref, gate_ref, out_ref = refs[2 * n_grp:]
    lses = [r[0] for r in lse_refs]
    mx = functools.reduce(jnp.maximum, lses)
    es = [jnp.exp(l - mx) for l in lses]
    inv = 1.0 / functools.reduce(jnp.add, es)
    o = None
    for gi in range(n_grp):
        wt = es[gi] * inv
        hi = wt.astype(BF16)
        lo = (wt - hi.astype(F32)).astype(BF16)
        wfull = jnp.dot(jnp.concatenate([hi, lo], axis=1), expand_ref[...],
                        preferred_element_type=F32)
        term = wfull * o_refs[gi][0].astype(F32)
        o = term if o is None else o + term
    y = o * zs_ref[0].astype(F32)
    _outproj_tail(y, w_ref, x_ref, gate_ref, None, out_ref)


def _outproj_mul_body(final, o_ref_in, zs_ref, w_ref, x_ref, gate_ref, *rest):
    fg_ref = rest[0] if final else None
    out_ref = rest[-1]
    y = o_ref_in[0].astype(F32) * zs_ref[0].astype(F32)
    _outproj_tail(y, w_ref, x_ref, gate_ref, fg_ref, out_ref)


def _tile_spec(tm, d):
    return pl.BlockSpec((1, tm, d), lambda bi, i: (bi, i, 0))


def _outproj_common_specs(tm, d):
    return [
        _resident((d, d)),
        _tile_spec(tm, d),
        pl.BlockSpec((1, 1, d), lambda bi, i: (bi, 0, 0)),
    ]


def _outproj_call(body, in_specs, args, x, tm, scratch=()):
    b, s, d = x.shape
    return pl.pallas_call(
        body,
        grid=(b, s // tm),
        in_specs=in_specs,
        out_specs=_tile_spec(tm, d),
        out_shape=jax.ShapeDtypeStruct((b, s, d), F32),
        scratch_shapes=list(scratch),
        compiler_params=_cparams(("parallel", "parallel")),
    )(*args)


def _fill_window(win_ref, prev_ref, cur_ref, next_ref, half, tq_blk):
    win_ref[0:half] = prev_ref[0]
    win_ref[half:half + tq_blk] = cur_ref[0]
    win_ref[half + tq_blk:2 * half + tq_blk] = next_ref[0]


def _band_mask(half, tq, j, n_sub, blk, n_blk):
    nk = tq + 2 * half
    row = lax.broadcasted_iota(jnp.int32, (tq, nk), 0)
    col = lax.broadcasted_iota(jnp.int32, (tq, nk), 1)
    delta = col - row
    mask = (delta >= 0) & (delta <= 2 * half)
    if j == 0:
        mask = mask & ((col >= half) | (blk > 0))
    if j == n_sub - 1:
        mask = mask & ((col < tq + half) | (blk < n_blk - 1))
    return mask


def _softmax_pv(s, mask, v, sink=None):
    s = jnp.where(mask, s, NEG_INF)
    m = jnp.max(s, axis=-1, keepdims=True)
    if sink is not None:
        m = jnp.maximum(m, sink)
    p = jnp.exp(s - m)
    l = jnp.sum(p, axis=-1, keepdims=True)
    if sink is not None:
        l = l + jnp.exp(sink - m)
    o = jnp.dot(p.astype(BF16), v, preferred_element_type=F32)
    return o / l, m, l


def _qk(q, k):
    return lax.dot_general(q, k, (((1,), (1,)), ((), ())), preferred_element_type=F32)


def _dil_attn_body(half, n_heads, q_ref, kp_ref, kc_ref, kn_ref, vp_ref, vc_ref, vn_ref,
                   o_ref, lse_ref, kwin_ref, vwin_ref):
    blk = pl.program_id(2)
    n_blk = pl.num_programs(2)
    tq_blk = q_ref.shape[1]
    tq = 2 * half
    n_sub = tq_blk // tq
    _fill_window(kwin_ref, kp_ref, kc_ref, kn_ref, half, tq_blk)
    _fill_window(vwin_ref, vp_ref, vc_ref, vn_ref, half, tq_blk)
    lane = lax.broadcasted_iota(jnp.int32, (tq, LANES), 1)
    for j in range(n_sub):
        mask = _band_mask(half, tq, j, n_sub, blk, n_blk)
        r0 = j * tq
        lse_tile = jnp.zeros((tq, LANES), F32)
        for h in range(n_heads):
            cs = slice(h * LANES, (h + 1) * LANES)
            s = _qk(q_ref[0, r0:r0 + tq, cs], kwin_ref[r0:r0 + tq + 2 * half, cs])
            o, m, l = _softmax_pv(s, mask, vwin_ref[r0:r0 + tq + 2 * half, cs])
            o_ref[0, r0:r0 + tq, cs] = o.astype(o_ref.dtype)
            lse_tile = jnp.where(lane == h, m + jnp.log(l), lse_tile)
        lse_ref[0, r0:r0 + tq, :] = lse_tile


def _dil_attention(q_all, k_all, v, grp, n_grp, rate, half, tq_blk):
    b, s, c = v.shape
    l = s // rate
    tq_blk = min(tq_blk, l)
    n_blk = l // tq_blk
    hb = tq_blk // half
    last_halo = l // half - 1
    view = lambda t: t.reshape(b, l, rate * t.shape[-1])

    def specs(col):
        cur = pl.BlockSpec((1, tq_blk, c), lambda bi, p, i: (bi, i, col(p)))
        prev = pl.BlockSpec((1, half, c), lambda bi, p, i: (bi, jnp.maximum(i * hb - 1, 0), col(p)))
        nxt = pl.BlockSpec((1, half, c),
                           lambda bi, p, i: (bi, jnp.minimum((i + 1) * hb, last_halo), col(p)))
        return cur, prev, nxt

    qc, qp, qn = specs(lambda p: p * n_grp + grp)
    vc, vp, vn = specs(lambda p: p)
    body = functools.partial(_dil_attn_body, half, c // LANES)
    o, lse = pl.pallas_call(
        body,
        grid=(b, rate, n_blk),
        in_specs=[qc, qp, qc, qn, vp, vc, vn],
        out_specs=[vc, pl.BlockSpec((1, tq_blk, LANES), lambda bi, p, i: (bi, i, p))],
        out_shape=[jax.ShapeDtypeStruct((b, l, rate * c), BF16),
                   jax.ShapeDtypeStruct((b, l, rate * LANES), F32)],
        scratch_shapes=[pltpu.VMEM((tq_blk + 2 * half, c), BF16),
                        pltpu.VMEM((tq_blk + 2 * half, c), BF16)],
        compiler_params=_cparams(("parallel", "parallel", "parallel")),
    )(view(q_all), view(k_all), view(k_all), view(k_all), view(v), view(v), view(v))
    return o.reshape(b, s, c), lse.reshape(b, s, LANES)


def _swa_attn_body(half, n_qv, q_per_kv, sink_ref, q_ref, kp_ref, kc_ref, kn_ref,
                   vp_ref, vc_ref, vn_ref, o_ref, kwin_ref, vwin_ref):
    blk = pl.program_id(1)
    n_blk = pl.num_programs(1)
    tq_blk = q_ref.shape[1]
    tq = 2 * half
    n_sub = tq_blk // tq
    _fill_window(kwin_ref, kp_ref, kc_ref, kn_ref, half, tq_blk)
    _fill_window(vwin_ref, vp_ref, vc_ref, vn_ref, half, tq_blk)
    lane = lax.broadcasted_iota(jnp.int32, (tq, LANES), 1)
    first = (lane % (LANES // 2)) < (LANES // 4)
    for j in range(n_sub):
        mask = _band_mask(half, tq, j, n_sub, blk, n_blk)
        r0 = j * tq
        for u in range(n_qv):
            kv = u // q_per_kv
            cs = slice(u * LANES, (u + 1) * LANES)
            ks = slice(kv * LANES, (kv + 1) * LANES)
            q = q_ref[0, r0:r0 + tq, cs]
            kw = kwin_ref[r0:r0 + tq + 2 * half, ks]
            vw = vwin_ref[r0:r0 + tq + 2 * half, ks]
            zero = jnp.zeros_like(q)
            oa, _, _ = _softmax_pv(_qk(jnp.where(first, q, zero), kw), mask, vw, sink_ref[2 * u])
            ob, _, _ = _softmax_pv(_qk(jnp.where(first, zero, q), kw), mask, vw, sink_ref[2 * u + 1])
            o_ref[0, r0:r0 + tq, cs] = jnp.where(lane < LANES // 2, oa, ob).astype(o_ref.dtype)


def _swa_attention(q, k, v, sink_pairs, half, tq_blk, q_per_kv):
    b, s, cq = q.shape
    ck = k.shape[-1]
    tq_blk = min(tq_blk, s)
    n_blk = s // tq_blk
    hb = tq_blk // half
    last_halo = s // half - 1
    qspec = pl.BlockSpec((1, tq_blk, cq), lambda bi, i: (bi, i, 0))
    cur = pl.BlockSpec((1, tq_blk, ck), lambda bi, i: (bi, i, 0))
    prev = pl.BlockSpec((1, half, ck), lambda bi, i: (bi, jnp.maximum(i * hb - 1, 0), 0))
    nxt = pl.BlockSpec((1, half, ck), lambda bi, i: (bi, jnp.minimum((i + 1) * hb, last_halo), 0))
    body = functools.partial(_swa_attn_body, half, cq // LANES, q_per_kv)
    return pl.pallas_call(
        body,
        grid=(b, n_blk),
        in_specs=[pl.BlockSpec(memory_space=pltpu.SMEM), qspec, prev, cur, nxt, prev, cur, nxt],
        out_specs=qspec,
        out_shape=jax.ShapeDtypeStruct((b, s, cq), BF16),
        scratch_shapes=[pltpu.VMEM((tq_blk + 2 * half, ck), BF16),
                        pltpu.VMEM((tq_blk + 2 * half, ck), BF16)],
        compiler_params=_cparams(("parallel", "parallel")),
    )(sink_pairs, q, k, k, k, v, v, v)


def _diff_attn_body(tk, lam_init, lam_ref, sg_ref, q_ref, k_ref, v_ref, o_ref,
                    vt_ref, qt_ref, st_ref, pt_ref, acc_ref):
    tq = q_ref.shape[1]
    n_kv = k_ref.shape[1] // tk

    @pl.when(pl.program_id(2) == 0)
    def _():
        for j in range(n_kv):
            vt_ref[j] = v_ref[0, j * tk:(j + 1) * tk, :].astype(F32).T.astype(BF16)

    row = lax.broadcasted_iota(jnp.int32, (LANES, tq), 0)
    first = (row % (LANES // 2)) < (LANES // 4)
    qt = q_ref[0].astype(F32).T
    qt_ref[:, 0:tq] = jnp.where(first, qt, 0.0).astype(BF16)
    qt_ref[:, tq:2 * tq] = jnp.where(first, 0.0, qt).astype(BF16)
    acc_ref[...] = jnp.zeros(acc_ref.shape, F32)

    def scores(kk, slot):
        k0 = pl.multiple_of(kk * tk, tk)
        st_ref[slot] = jnp.dot(k_ref[0, pl.ds(k0, tk), :], qt_ref[...],
                               preferred_element_type=F32)

    def fold(x, op):
        while x.shape[0] > 8:
            half = x.shape[0] // 2
            x = op(x[:half], x[half:])
        return x

    def softmax(slot, m_old, l_old):
        st = st_ref[slot]
        m_new = jnp.maximum(m_old, jnp.max(fold(st, jnp.maximum), axis=0, keepdims=True))
        alpha = jnp.exp2(m_old - m_new)
        pt = jnp.exp2(st - m_new)
        pt_ref[slot] = pt.astype(BF16)
        l_new = alpha * l_old + jnp.sum(fold(pt, jnp.add), axis=0, keepdims=True)
        return m_new, l_new, alpha

    def half_step(kk, slot, state, more_scores=True):
        m, l, alpha = state
        pv = jnp.dot(vt_ref[kk], pt_ref[slot], preferred_element_type=F32)
        if more_scores:
            scores(kk + 2, slot)
        state = softmax(1 - slot, m, l)
        acc_ref[...] = alpha * acc_ref[...] + pv
        return state

    def step(i, state):
        state = half_step(2 * i, 0, state)
        return half_step(2 * i + 1, 1, state)

    scores(0, 0)
    scores(1, 1)
    m_init = jnp.full((1, 2 * tq), NEG_INF, F32)
    l_init = jnp.zeros((1, 2 * tq), F32)
    state = softmax(0, m_init, l_init)
    state = lax.fori_loop(0, n_kv // 2 - 1, step, state)
    _, l_fin, alpha = half_step(n_kv - 2, 0, state, more_scores=False)
    acc_ref[...] = alpha * acc_ref[...] + jnp.dot(vt_ref[n_kv - 1], pt_ref[1],
                                                  preferred_element_type=F32)

    lv = lam_ref[...]
    lam = (jnp.exp(jnp.sum(lv[0:1] * lv[1:2], axis=-1, keepdims=True))
           - jnp.exp(jnp.sum(lv[2:3] * lv[3:4], axis=-1, keepdims=True)) + lam_init)
    o_all = acc_ref[...] / l_fin
    ot = o_all[:, 0:tq] - lam * o_all[:, tq:2 * tq]
    ms = jnp.mean(ot * ot, axis=0, keepdims=True)
    ot = ot * lax.rsqrt(ms + SUBLN_EPS) * sg_ref[...] * (1.0 - lam_init)
    o_ref[0] = ot.T.astype(o_ref.dtype)


def _diff_attention(q, k, v, lam_vecs, subln_g, lam_init, tq, tk):
    b, s, c = q.shape
    tq = min(tq, s)
    tk = min(tk, s)
    n_heads = c // LANES
    body = functools.partial(_diff_attn_body, tk, lam_init)
    qspec = pl.BlockSpec((1, tq, LANES), lambda bi, h, i: (bi, i, h))
    kvspec = pl.BlockSpec((1, s, LANES), lambda bi, h, i: (bi, 0, h))
    sg = jnp.broadcast_to(subln_g[:, None], (LANES, tq))
    return pl.pallas_call(
        body,
        grid=(b, n_heads, s // tq),
        in_specs=[_resident(lam_vecs.shape), _resident((LANES, tq)), qspec, kvspec, kvspec],
        out_specs=qspec,
        out_shape=jax.ShapeDtypeStruct((b, s, c), BF16),
        scratch_shapes=[pltpu.VMEM((s // tk, LANES, tk), BF16),
                        pltpu.VMEM((LANES, 2 * tq), BF16),
                        pltpu.VMEM((2, tk, 2 * tq), F32),
                        pltpu.VMEM((2, tk, 2 * tq), BF16),
                        pltpu.VMEM((LANES, 2 * tq), F32)],
        compiler_params=_cparams(("parallel", "parallel", "arbitrary")),
    )(lam_vecs, sg, q, k, v)


def _rope_tables(seq, dim, scale):
    inv = ROPE_THETA ** (-jnp.arange(0, dim, 2, dtype=F32) / dim)
    ang = jnp.arange(seq, dtype=F32)[:, None] * inv[None, :]
    reps = (LANES // 2) // (dim // 2)
    cos = jnp.tile(jnp.cos(ang), (1, 2 * reps))
    sin = jnp.tile(jnp.sin(ang), (1, reps))
    sin = jnp.concatenate([-sin, sin], axis=1)
    return cos, sin, cos * scale, sin * scale


def _pair_interleave(a0, b0, half_dim):
    r = np.arange(half_dim)
    return np.concatenate([a0 + r, b0 + r, a0 + half_dim + r, b0 + half_dim + r])


def _cast_w(w, cols=None):
    if cols is not None:
        w = w[:, cols]
    return w.astype(BF16)


def _conv_layer(x, shift, scale, gate, g, w_in, conv_k, w_out, tm):
    b, s, d = x.shape
    e = w_out.shape[0]
    ops = [("mul", (e, 2 * e), e, 0, 0, None),
           ("mulsilu", (0, 3 * e), e, 1, 0, None)]
    t, gz = _inproj(x, shift, scale, g, _cast_w(w_in), [], [e, e], ops, tm)
    halo = 16
    hb = tm // halo
    last = s // halo - 1
    specs = [
        pl.BlockSpec((1, halo, e), lambda bi, i: (bi, jnp.maximum(i * hb - 1, 0), 0)),
        _tile_spec(tm, e),
        pl.BlockSpec((1, halo, e), lambda bi, i: (bi, jnp.minimum((i + 1) * hb, last), 0)),
        _tile_spec(tm, e),
        _resident(conv_k.shape),
    ] + _outproj_common_specs(tm, d)
    body = functools.partial(_outproj_conv_body, halo)
    return _outproj_call(body, specs, (t, t, t, gz, conv_k, _cast_w(w_out), x, gate), x, tm,
                         scratch=[pltpu.VMEM((tm + 2 * halo, e), F32)])


def _dil_layer(x, shift, scale, gate, g, w_in, w_out, tm):
    b, s, d = x.shape
    n_grp = len(DIL_RATES)
    e = w_out.shape[0]
    dh = e // DIL_HEADS
    cos, sin, cos_q, sin_q = _rope_tables(s, dh, dh ** -0.5)
    ops = [("rope", (0,), n_grp * e, 0, 0, (2, 3)),
           ("rope", (n_grp * e,), n_grp * e, 1, 0, (0, 1)),
           ("plain", (2 * n_grp * e,), e, 2, 0, None),
           ("silu", (2 * n_grp * e + e,), e, 3, 0, None)]
    q, k, v, zs = _inproj(x, shift, scale, g, _cast_w(w_in), [cos, sin, cos_q, sin_q],
                          [n_grp * e, n_grp * e, e, e], ops, tm)
    os_, lses = [], []
    for gi, rate in enumerate(DIL_RATES):
        half = DIL_WINDOWS[gi] // (2 * rate)
        o, lse = _dil_attention(q, k, v, gi, n_grp, rate, half, 512)
        os_.append(o)
        lses.append(lse)
    rows = np.arange(2 * LANES)[:, None] % LANES
    expand = jnp.asarray(rows == (np.arange(e)[None, :] // dh), dtype=BF16)
    specs = ([_tile_spec(tm, e)] * n_grp + [_tile_spec(tm, LANES)] * n_grp
             + [_tile_spec(tm, e), _resident(expand.shape)] + _outproj_common_specs(tm, d))
    body = functools.partial(_outproj_dil_body, n_grp)
    return _outproj_call(body, specs, (*os_, *lses, zs, expand, _cast_w(w_out), x, gate), x, tm)


def _swa_layer(x, shift, scale, gate, g, w_in, sink, w_out, tm):
    b, s, d = x.shape
    hq, hk, dh = SWA_Q_HEADS, SWA_KV_HEADS, SWA_HEAD_DIM
    grp = hq // hk
    pairs = [((2 * p) * grp + gi, (2 * p + 1) * grp + gi) for p in range(hk // 2) for gi in range(grp)]
    q_cols = np.concatenate([_pair_interleave(a * dh, b_ * dh, dh // 2) for a, b_ in pairs])
    k_cols = hq * dh + np.concatenate(
        [_pair_interleave(2 * p * dh, (2 * p + 1) * dh, dh // 2) for p in range(hk // 2)])
    v_cols = hq * dh + hk * dh + np.arange(hk * dh)
    head_cols = np.concatenate([np.concatenate([a * dh + np.arange(dh), b_ * dh + np.arange(dh)])
                                for a, b_ in pairs])
    z_cols = hq * dh + 2 * hk * dh + head_cols
    w = _cast_w(w_in, np.concatenate([q_cols, k_cols, v_cols, z_cols]))
    sink_pairs = sink[np.array([h for pr in pairs for h in pr])]
    cos, sin, cos_q, sin_q = _rope_tables(s, dh, dh ** -0.5)
    nq, nk = hq * dh, hk * dh
    ops = [("rope", (0,), nq, 0, 0, (2, 3)),
           ("rope", (nq,), nk, 1, 0, (0, 1)),
           ("plain", (nq + nk,), nk, 2, 0, None),
           ("silu", (nq + 2 * nk,), nq, 3, 0, None)]
    q, k, v, zs = _inproj(x, shift, scale, g, w, [cos, sin, cos_q, sin_q], [nq, nk, nk, nq], ops, tm)
    o = _swa_attention(q, k, v, sink_pairs, SWA_HALF, 512, grp)
    specs = [_tile_spec(tm, nq), _tile_spec(tm, nq)] + _outproj_common_specs(tm, d)
    body = functools.partial(_outproj_mul_body, False)
    return _outproj_call(body, specs, (o, zs, _cast_w(w_out[head_cols]), x, gate), x, tm)


def _diff_layer(x, shift, scale, gate, g, w_in, lam_vecs, subln_g, w_out, final_g, layer_idx, tm):
    b, s, d = x.shape
    nh, dh = DIFF_HEADS, DIFF_HEAD_DIM
    e = nh * 2 * dh
    qk_cols = np.concatenate([_pair_interleave(h * 2 * dh, h * 2 * dh + dh, dh // 2) for h in range(nh)])
    cols = np.concatenate([qk_cols, e + qk_cols, 2 * e + np.arange(2 * e)])
    w = _cast_w(w_in, cols)
    cos, sin, cos_q, sin_q = _rope_tables(s, dh, dh ** -0.5 * math.log2(math.e))
    ops = [("rope", (0,), e, 0, 0, (2, 3)),
           ("rope", (e,), e, 1, 0, (0, 1)),
           ("plain", (2 * e,), e, 2, 0, None),
           ("silu", (3 * e,), e, 3, 0, None)]
    q, k, v, zs = _inproj(x, shift, scale, g, w, [cos, sin, cos_q, sin_q], [e, e, e, e], ops, tm)
    lam_init = 0.8 - 0.6 * math.exp(-0.3 * layer_idx)
    o = _diff_attention(q, k, v, lam_vecs, subln_g, lam_init, 256, 512)
    specs = ([_tile_spec(tm, e), _tile_spec(tm, e)] + _outproj_common_specs(tm, d)
             + ([_resident((1, d))] if final_g is not None else []))
    body = functools.partial(_outproj_mul_body, final_g is not None)
    args = (o, zs, _cast_w(w_out), x, gate) + ((final_g.reshape(1, d),) if final_g is not None else ())
    return _outproj_call(body, specs, args, x, tm)


def kernel(x, c, norm_g, w_mod, b_mod, conv_w_in, conv_k, conv_w_out, dil_w_in, dil_w_out,
           swa_w_in, swa_sink, swa_w_out, diff_w_in, diff_lambda, diff_subln_g, diff_w_out,
           final_g):
    b, s, d = x.shape
    depth = norm_g.shape[0]
    tm = min(512, s)
    mod = _modulation(c, w_mod, b_mod)
    for i in range(depth):
        kind, j = i % N_MIXERS, i // N_MIXERS
        shift, scale, gate = (mod[i, t][:, None, :] for t in range(3))
        g = norm_g[i].reshape(1, d)
        last = i == depth - 1
        if kind == 0:
            x = _conv_layer(x, shift, scale, gate, g, conv_w_in[j], conv_k[j], conv_w_out[j], tm)
        elif kind == 1:
            x = _dil_layer(x, shift, scale, gate, g, dil_w_in[j], dil_w_out[j], tm)
        elif kind == 2:
            x = _swa_layer(x, shift, scale, gate, g, swa_w_in[j], swa_sink[j], swa_w_out[j], tm)
        else:
            x = _diff_layer(x, shift, scale, gate, g, diff_w_in[j], diff_lambda[j], diff_subln_g[j],
                            diff_w_out[j], final_g if last else None, i, tm)
        if last and kind != 3:
            raise NotImplementedError("final RMSNorm is fused into the differential layer")
    return x
```

```python
import functools
import math

import numpy as np
import jax
import jax.numpy as jnp
from jax import lax
from jax.experimental import pallas as pl
from jax.experimental.pallas import tpu as pltpu

F32 = jnp.float32
BF16 = jnp.bfloat16

LANES = 128
BF16_ROWS = 16
VMEM_LIMIT = 56 * 1024 * 1024

N_MIXERS = 4
CONV_WIDTH = 3
DIL_WINDOWS = (128, 512, 2048)
DIL_RATES = (1, 4, 16)
DIL_HEADS = 8
SWA_HALF = 128
SWA_Q_HEADS = 16
SWA_KV_HEADS = 4
SWA_HEAD_DIM = 64
DIFF_HEADS = 8
DIFF_HEAD_DIM = 64
ROPE_THETA = 10000.0
NORM_EPS = 1e-6
SUBLN_EPS = 1e-5
NEG_INF = -1e30


def _cparams(sem):
    return pltpu.CompilerParams(dimension_semantics=sem, vmem_limit_bytes=VMEM_LIMIT)


def _resident(shape):
    nd = len(shape)
    return pl.BlockSpec(shape, lambda *_: (0,) * nd)


def _mod_body(c_ref, w_ref, b_ref, o_ref):
    c = c_ref[...]
    sc = c * jax.nn.sigmoid(c)
    o_ref[0, 0] = jnp.dot(sc, w_ref[0], precision=lax.Precision.HIGHEST,
                          preferred_element_type=F32) + b_ref[0, 0]


def _modulation(c, w_mod, b_mod):
    depth, d, _ = w_mod.shape
    b = c.shape[0]
    rows = -(-b // 8) * 8
    cp = jnp.pad(c, ((0, rows - b), (0, 0)))
    out = pl.pallas_call(
        _mod_body,
        grid=(depth, 3),
        in_specs=[
            pl.BlockSpec((rows, d), lambda i, j: (0, 0)),
            pl.BlockSpec((1, d, d), lambda i, j: (i, 0, j)),
            pl.BlockSpec((1, 1, 1, d), lambda i, j: (i, j, 0, 0)),
        ],
        out_specs=pl.BlockSpec((1, 1, rows, d), lambda i, j: (i, j, 0, 0)),
        out_shape=jax.ShapeDtypeStruct((depth, 3, rows, d), F32),
        compiler_params=_cparams(("arbitrary", "arbitrary")),
    )(cp, w_mod, b_mod.reshape(depth, 3, 1, d))
    return out[:, :, :b]


def _silu(a):
    return a * jax.nn.sigmoid(a)


def _inproj_body(ops, n_tab, n_out, chunk, x_ref, shift_ref, scale_ref, g_ref, w_ref, *rest):
    tabs = rest[:n_tab]
    outs = rest[n_tab:n_tab + n_out]
    h_ref, tmp_ref = rest[n_tab + n_out:]
    tm = x_ref.shape[1]
    x = x_ref[0]
    ms = jnp.mean(x * x, axis=-1, keepdims=True)
    y = x * lax.rsqrt(ms + NORM_EPS) * g_ref[...]
    h_ref[...] = (y * (1.0 + scale_ref[0]) + shift_ref[0]).astype(BF16)

    def proj(c0, n):
        return jnp.dot(h_ref[...], w_ref[:, c0:c0 + n], preferred_element_type=F32)

    for kind, srcs, ncols, dests, tab in ops:
        for c in range(0, ncols, chunk):
            n = min(chunk, ncols - c)
            if kind == "plain":
                r = proj(srcs[0] + c, n)
            elif kind == "silu":
                r = _silu(proj(srcs[0] + c, n))
            elif kind == "mul":
                r = proj(srcs[0] + c, n) * proj(srcs[1] + c, n)
            elif kind == "mulsilu":
                r = proj(srcs[0] + c, n) * _silu(proj(srcs[1] + c, n))
            elif kind == "rope":
                a = proj(srcs[0] + c, n)
                cos = tabs[tab[0]][...]
                sin = tabs[tab[1]][...]
                parts = []
                for j in range(n // LANES):
                    aj = a[:, j * LANES:(j + 1) * LANES]
                    parts.append(aj * cos + pltpu.roll(aj, LANES // 2, 1) * sin)
                r = parts[0] if len(parts) == 1 else jnp.concatenate(parts, axis=1)
            else:
                raise ValueError(kind)
            staged = False
            for oi, ooff, rate in dests:
                o_ref = outs[oi]
                cols = slice(ooff + c, ooff + c + n)
                if rate == 1:
                    o_ref[0, :, cols] = r.astype(o_ref.dtype)
                    continue
                if not staged:
                    for j in range(n // LANES):
                        tmp_ref[j] = r[:, j * LANES:(j + 1) * LANES]
                    staged = True
                for p in range(rate):
                    for j in range(n // LANES):
                        o_ref[0, p, :, ooff + c + j * LANES:ooff + c + (j + 1) * LANES] = (
                            tmp_ref[j, pl.ds(p, tm // rate, stride=rate), :].astype(o_ref.dtype))


def _inproj(x, shift, scale, g, w, tables, outs, ops, tm, chunk=512):
    b, s, d = x.shape
    n = w.shape[1]
    in_specs = [
        pl.BlockSpec((1, tm, d), lambda bi, i: (bi, i, 0)),
        pl.BlockSpec((1, 1, d), lambda bi, i: (bi, 0, 0)),
        pl.BlockSpec((1, 1, d), lambda bi, i: (bi, 0, 0)),
        _resident((1, d)),
        pl.BlockSpec((d, n), lambda bi, i: (0, 0), pipeline_mode=pl.Buffered(1)),
    ] + [pl.BlockSpec((tm, LANES), lambda bi, i: (i, 0)) for _ in tables]
    out_specs, out_shape = [], []
    for wd, rate in outs:
        if rate == 1:
            out_specs.append(pl.BlockSpec((1, tm, wd), lambda bi, i: (bi, i, 0)))
            out_shape.append(jax.ShapeDtypeStruct((b, s, wd), BF16))
        else:
            out_specs.append(pl.BlockSpec((1, rate, tm // rate, wd), lambda bi, i: (bi, 0, i, 0)))
            out_shape.append(jax.ShapeDtypeStruct((b, rate, s // rate, wd), BF16))
    body = functools.partial(_inproj_body, tuple(ops), len(tables), len(outs), chunk)
    return pl.pallas_call(
        body,
        grid=(b, s // tm),
        in_specs=in_specs,
        out_specs=out_specs,
        out_shape=out_shape,
        scratch_shapes=[pltpu.VMEM((tm, d), BF16), pltpu.VMEM((chunk // LANES, tm, LANES), F32)],
        compiler_params=_cparams(("parallel", "parallel")),
    )(x, shift, scale, g, w, *tables)


def _outproj_tail(y, w_ref, x_ref, gate_ref, fg_ref, o_ref):
    acc = jnp.dot(y.astype(BF16), w_ref[...], preferred_element_type=F32)
    xn = x_ref[0] + gate_ref[0] * acc
    if fg_ref is not None:
        ms = jnp.mean(xn * xn, axis=-1, keepdims=True)
        xn = xn * lax.rsqrt(ms + NORM_EPS) * fg_ref[...]
    o_ref[0] = xn


def _outproj_conv_body(halo, t_prev_ref, t_ref, t_next_ref, g_ref, ck_ref, w_ref, x_ref,
                       gate_ref, o_ref, win_ref):
    i = pl.program_id(1)
    tm = t_ref.shape[1]
    has_prev = (i > 0).astype(F32)
    has_next = (i < pl.num_programs(1) - 1).astype(F32)
    win_ref[0:halo] = t_prev_ref[0].astype(F32) * has_prev
    win_ref[halo:halo + tm] = t_ref[0].astype(F32)
    win_ref[halo + tm:2 * halo + tm] = t_next_ref[0].astype(F32) * has_next
    ck = ck_ref[...]
    conv = (win_ref[halo - 1:halo - 1 + tm] * ck[0:1]
            + win_ref[halo:halo + tm] * ck[1:2]
            + win_ref[halo + 1:halo + 1 + tm] * ck[2:3])
    y = g_ref[0].astype(F32) * conv
    _outproj_tail(y, w_ref, x_ref, gate_ref, None, o_ref)


def _interleave(src_ref, stage_ref):
    rate = src_ref.shape[1]
    if rate == 1:
        return src_ref[0, 0].astype(F32)
    n = src_ref.shape[2]
    n_col = src_ref.shape[3] // LANES
    for p in range(rate):
        for j in range(n_col):
            stage_ref[j, pl.ds(p, n, stride=rate), :] = (
                src_ref[0, p, :, j * LANES:(j + 1) * LANES].astype(F32))
    if n_col == 1:
        return stage_ref[0]
    return jnp.concatenate([stage_ref[j] for j in range(n_col)], axis=1)


def _outproj_dil_body(n_grp, *refs):
    o_refs = refs[:n_grp]
    lse_refs = refs[n_grp:2 * n_grp]
    zs_ref, expand_ref, w_ref, x_ref, gate_ref, out_ref = refs[2 * n_grp:2 * n_grp + 6]
    o_stage = refs[2 * n_grp + 6:3 * n_grp + 6]
    lse_stage = refs[3 * n_grp + 6:]
    lses = [_interleave(r, st) for r, st in zip(lse_refs, lse_stage)]
    mx = functools.reduce(jnp.maximum, lses)
    es = [jnp.exp(l - mx) for l in lses]
    inv = 1.0 / functools.reduce(jnp.add, es)
    o = None
    for gi in range(n_grp):
        wt = es[gi] * inv
        hi = wt.astype(BF16)
        lo = (wt - hi.astype(F32)).astype(BF16)
        wfull = jnp.dot(jnp.concatenate([hi, lo], axis=1), expand_ref[...],
                        preferred_element_type=F32)
        term = wfull * _interleave(o_refs[gi], o_stage[gi])
        o = term if o is None else o + term
    y = o * zs_ref[0].astype(F32)
    _outproj_tail(y, w_ref, x_ref, gate_ref, None, out_ref)


def _outproj_mul_body(final, o_ref_in, zs_ref, w_ref, x_ref, gate_ref, *rest):
    fg_ref = rest[0] if final else None
    out_ref = rest[-1]
    y = o_ref_in[0].astype(F32) * zs_ref[0].astype(F32)
    _outproj_tail(y, w_ref, x_ref, gate_ref, fg_ref, out_ref)


def _tile_spec(tm, d):
    return pl.BlockSpec((1, tm, d), lambda bi, i: (bi, i, 0))


def _outproj_common_specs(tm, d):
    return [
        _resident((d, d)),
        _tile_spec(tm, d),
        pl.BlockSpec((1, 1, d), lambda bi, i: (bi, 0, 0)),
    ]


def _outproj_call(body, in_specs, args, x, tm, scratch=()):
    b, s, d = x.shape
    return pl.pallas_call(
        body,
        grid=(b, s // tm),
        in_specs=in_specs,
        out_specs=_tile_spec(tm, d),
        out_shape=jax.ShapeDtypeStruct((b, s, d), F32),
        scratch_shapes=list(scratch),
        compiler_params=_cparams(("parallel", "parallel")),
    )(*args)


def _fill_window(win_ref, prev_ref, cur_ref, next_ref, half, tq_blk):
    win_ref[0:half] = prev_ref[0]
    win_ref[half:half + tq_blk] = cur_ref[0]
    win_ref[half + tq_blk:2 * half + tq_blk] = next_ref[0]


def _band_mask(half, tq, j, n_sub, blk, n_blk):
    nk = tq + 2 * half
    row = lax.broadcasted_iota(jnp.int32, (tq, nk), 0)
    col = lax.broadcasted_iota(jnp.int32, (tq, nk), 1)
    delta = col - row
    mask = (delta >= 0) & (delta <= 2 * half)
    if j == 0:
        mask = mask & ((col >= half) | (blk > 0))
    if j == n_sub - 1:
        mask = mask & ((col < tq + half) | (blk < n_blk - 1))
    return mask


def _softmax_pv(s, mask, v, sink=None):
    s = jnp.where(mask, s, NEG_INF)
    m = jnp.max(s, axis=-1, keepdims=True)
    if sink is not None:
        m = jnp.maximum(m, sink)
    p = jnp.exp(s - m)
    l = jnp.sum(p, axis=-1, keepdims=True)
    if sink is not None:
        l = l + jnp.exp(sink - m)
    o = jnp.dot(p.astype(BF16), v, preferred_element_type=F32)
    return o / l, m, l


def _qk(q, k):
    return lax.dot_general(q, k, (((1,), (1,)), ((), ())), preferred_element_type=F32)


def _dil_attn_body(half, n_heads, q_ref, kp_ref, kc_ref, kn_ref, vp_ref, vc_ref, vn_ref,
                   o_ref, lse_ref, kwin_ref, vwin_ref):
    blk = pl.program_id(2)
    n_blk = pl.num_programs(2)
    tq_blk = q_ref.shape[1]
    tq = 2 * half
    n_sub = tq_blk // tq
    _fill_window(kwin_ref, kp_ref, kc_ref, kn_ref, half, tq_blk)
    _fill_window(vwin_ref, vp_ref, vc_ref, vn_ref, half, tq_blk)
    lane = lax.broadcasted_iota(jnp.int32, (tq, LANES), 1)
    for j in range(n_sub):
        mask = _band_mask(half, tq, j, n_sub, blk, n_blk)
        r0 = j * tq
        lse_tile = jnp.zeros((tq, LANES), F32)
        for h in range(n_heads):
            cs = slice(h * LANES, (h + 1) * LANES)
            s = _qk(q_ref[0, r0:r0 + tq, cs], kwin_ref[r0:r0 + tq + 2 * half, cs])
            o, m, l = _softmax_pv(s, mask, vwin_ref[r0:r0 + tq + 2 * half, cs])
            o_ref[0, r0:r0 + tq, cs] = o.astype(o_ref.dtype)
            lse_tile = jnp.where(lane == h, m + jnp.log(l), lse_tile)
        lse_ref[0, r0:r0 + tq, :] = lse_tile


def _dil_attention(q, k, v, half, tq_blk):
    b, rate, l, c = v.shape
    tq_blk = min(tq_blk, l)
    n_blk = l // tq_blk
    hb = tq_blk // half
    last_halo = l // half - 1
    cur = pl.BlockSpec((1, None, tq_blk, c), lambda bi, p, i: (bi, p, i, 0))
    prev = pl.BlockSpec((1, None, half, c), lambda bi, p, i: (bi, p, jnp.maximum(i * hb - 1, 0), 0))
    nxt = pl.BlockSpec((1, None, half, c),
                       lambda bi, p, i: (bi, p, jnp.minimum((i + 1) * hb, last_halo), 0))
    body = functools.partial(_dil_attn_body, half, c // LANES)
    return pl.pallas_call(
        body,
        grid=(b, rate, n_blk),
        in_specs=[cur, prev, cur, nxt, prev, cur, nxt],
        out_specs=[cur, pl.BlockSpec((1, None, tq_blk, LANES), lambda bi, p, i: (bi, p, i, 0))],
        out_shape=[jax.ShapeDtypeStruct((b, rate, l, c), BF16),
                   jax.ShapeDtypeStruct((b, rate, l, LANES), F32)],
        scratch_shapes=[pltpu.VMEM((tq_blk + 2 * half, c), BF16),
                        pltpu.VMEM((tq_blk + 2 * half, c), BF16)],
        compiler_params=_cparams(("parallel", "parallel", "parallel")),
    )(q, k, k, k, v, v, v)


def _swa_attn_body(half, n_pairs, q_per_kv, sink_ref, q_ref, kp_ref, kc_ref, kn_ref,
                   vp_ref, vc_ref, vn_ref, o_ref, kwin_ref, vwin_ref, s_ref):
    blk = pl.program_id(1)
    n_blk = pl.num_programs(1)
    tq_blk = q_ref.shape[1]
    tq = half
    nk = tq + 2 * half
    n_sub = tq_blk // tq
    n_stack = 2 * q_per_kv
    _fill_window(kwin_ref, kp_ref, kc_ref, kn_ref, half, tq_blk)
    for p in range(n_pairs):
        vs = slice(p * LANES, (p + 1) * LANES)
        ws = slice(2 * p * LANES, (2 * p + 1) * LANES)
        vwin_ref[0:half, ws] = vp_ref[0, :, vs]
        vwin_ref[half:half + tq_blk, ws] = vc_ref[0, :, vs]
        vwin_ref[half + tq_blk:2 * half + tq_blk, ws] = vn_ref[0, :, vs]
        vwin_ref[:, (2 * p + 1) * LANES:(2 * p + 2) * LANES] = jnp.ones(
            (tq_blk + 2 * half, LANES), BF16)
    lane = lax.broadcasted_iota(jnp.int32, (tq, LANES), 1)
    first = (lane % (LANES // 2)) < (LANES // 4)
    units = [(j, p) for j in range(n_sub) for p in range(n_pairs)]

    def scores(i):
        j, p = units[i]
        r0 = j * tq
        qs = []
        for g in range(q_per_kv):
            u = p * q_per_kv + g
            q = q_ref[0, r0:r0 + tq, u * LANES:(u + 1) * LANES]
            zero = jnp.zeros_like(q)
            qs += [jnp.where(first, q, zero), jnp.where(first, zero, q)]
        s_ref[i % 2] = _qk(jnp.concatenate(qs, axis=0),
                           kwin_ref[r0:r0 + nk, p * LANES:(p + 1) * LANES])

    scores(0)
    for i, (j, p) in enumerate(units):
        if i + 1 < len(units):
            scores(i + 1)
        r0 = j * tq
        mask = _band_mask(half, tq, j, n_sub, blk, n_blk)[None]
        sinks = []
        for g in range(q_per_kv):
            u = p * q_per_kv + g
            sinks += [jnp.full((tq, 1), sink_ref[2 * u], F32),
                      jnp.full((tq, 1), sink_ref[2 * u + 1], F32)]
        sink = jnp.concatenate(sinks, axis=0)
        s = jnp.where(mask, s_ref[i % 2].reshape(n_stack, tq, nk), NEG_INF).reshape(n_stack * tq, nk)
        m = jnp.maximum(jnp.max(s, axis=-1, keepdims=True), sink)
        pr = jnp.exp2((s - m).astype(BF16))
        out = jnp.dot(pr, vwin_ref[r0:r0 + nk, 2 * p * LANES:(2 * p + 2) * LANES],
                      preferred_element_type=F32)
        o = out[:, 0:LANES] / (out[:, LANES:2 * LANES] + jnp.exp2(sink - m))
        for g in range(q_per_kv):
            u = p * q_per_kv + g
            oa = o[2 * g * tq:(2 * g + 1) * tq]
            ob = o[(2 * g + 1) * tq:(2 * g + 2) * tq]
            o_ref[0, r0:r0 + tq, u * LANES:(u + 1) * LANES] = jnp.where(
                lane < LANES // 2, oa, ob).astype(o_ref.dtype)


def _swa_attention(q, k, v, sink_pairs, half, tq_blk, q_per_kv):
    b, s, cq = q.shape
    ck = k.shape[-1]
    tq_blk = min(tq_blk, s)
    n_blk = s // tq_blk
    hb = tq_blk // half
    last_halo = s // half - 1
    qspec = pl.BlockSpec((1, tq_blk, cq), lambda bi, i: (bi, i, 0))
    cur = pl.BlockSpec((1, tq_blk, ck), lambda bi, i: (bi, i, 0))
    prev = pl.BlockSpec((1, half, ck), lambda bi, i: (bi, jnp.maximum(i * hb - 1, 0), 0))
    nxt = pl.BlockSpec((1, half, ck), lambda bi, i: (bi, jnp.minimum((i + 1) * hb, last_halo), 0))
    body = functools.partial(_swa_attn_body, half, ck // LANES, q_per_kv)
    return pl.pallas_call(
        body,
        grid=(b, n_blk),
        in_specs=[pl.BlockSpec(memory_space=pltpu.SMEM), qspec, prev, cur, nxt, prev, cur, nxt],
        out_specs=qspec,
        out_shape=jax.ShapeDtypeStruct((b, s, cq), BF16),
        scratch_shapes=[pltpu.VMEM((tq_blk + 2 * half, ck), BF16),
                        pltpu.VMEM((tq_blk + 2 * half, 2 * ck), BF16),
                        pltpu.VMEM((2, 2 * q_per_kv * half, 3 * half), F32)],
        compiler_params=_cparams(("parallel", "parallel")),
    )(sink_pairs, q, k, k, k, v, v, v)


def _diff_attn_body(tk, lam_init, lam_ref, sg_ref, q_ref, k_ref, v_ref, o_ref,
                    vt_ref, qt_ref, st_ref, pt_ref, acc_ref):
    tq = q_ref.shape[1]
    n_kv = k_ref.shape[1] // tk

    @pl.when(pl.program_id(2) == 0)
    def _():
        for j in range(n_kv):
            vt_ref[j, 0:LANES] = v_ref[0, j * tk:(j + 1) * tk, :].astype(F32).T.astype(BF16)
            vt_ref[j, LANES:] = jnp.ones((vt_ref.shape[1] - LANES, tk), BF16)

    row = lax.broadcasted_iota(jnp.int32, (LANES, tq), 0)
    first = (row % (LANES // 2)) < (LANES // 4)
    qt = q_ref[0].astype(F32).T
    qt_ref[:, 0:tq] = jnp.where(first, qt, 0.0).astype(BF16)
    qt_ref[:, tq:2 * tq] = jnp.where(first, 0.0, qt).astype(BF16)
    acc_ref[...] = jnp.zeros(acc_ref.shape, F32)

    def fold(x, op):
        while x.shape[0] > 8:
            half = x.shape[0] // 2
            x = op(x[:half], x[half:])
        return x

    def scores(kk, slot):
        k0 = kk * tk if isinstance(kk, int) else pl.multiple_of(kk * tk, tk)
        st_ref[slot] = jnp.dot(k_ref[0, pl.ds(k0, tk), :], qt_ref[...],
                               preferred_element_type=F32)

    def softmax(slot, m_old):
        st = st_ref[slot]
        m_new = jnp.maximum(m_old, jnp.max(fold(st, jnp.maximum), axis=0, keepdims=True))
        pt_ref[slot] = jnp.exp2((st - m_new).astype(BF16))
        return m_new, jnp.exp2(m_old - m_new)

    depth = st_ref.shape[0]

    def tile_step(kk, slot, state):
        m, alpha = state
        pv = jnp.dot(vt_ref[kk], pt_ref[slot], preferred_element_type=F32)
        ahead = kk + depth
        scores(min(ahead, n_kv - 1) if isinstance(kk, int) else jnp.minimum(ahead, n_kv - 1), slot)
        state = softmax((slot + 1) % depth, m)
        acc_ref[...] = alpha * acc_ref[...] + pv
        return state

    def step(i, state):
        for j in range(depth):
            state = tile_step(depth * i + j, j, state)
        return state

    for j in range(depth):
        scores(j, j)
    state = softmax(0, jnp.full((1, 2 * tq), NEG_INF, F32))
    n_loop = (n_kv - 1) // depth
    state = lax.fori_loop(0, n_loop, step, state)
    for kk in range(n_loop * depth, n_kv - 1):
        state = tile_step(kk, kk % depth, state)
    _, alpha = state
    acc_ref[...] = alpha * acc_ref[...] + jnp.dot(vt_ref[n_kv - 1], pt_ref[(n_kv - 1) % depth],
                                                  preferred_element_type=F32)

    lv = lam_ref[...]
    lam = (jnp.exp(jnp.sum(lv[0:1] * lv[1:2], axis=-1, keepdims=True))
           - jnp.exp(jnp.sum(lv[2:3] * lv[3:4], axis=-1, keepdims=True)) + lam_init)
    o_all = acc_ref[0:LANES] / acc_ref[LANES:LANES + 1]
    ot = o_all[:, 0:tq] - lam * o_all[:, tq:2 * tq]
    ms = jnp.mean(ot * ot, axis=0, keepdims=True)
    ot = ot * lax.rsqrt(ms + SUBLN_EPS) * sg_ref[...] * (1.0 - lam_init)
    o_ref[0] = ot.T.astype(o_ref.dtype)


def _diff_attention(q, k, v, lam_vecs, subln_g, lam_init, tq, tk, depth=3):
    b, s, c = q.shape
    tq = min(tq, s)
    tk = min(tk, s)
    assert s // tk > depth
    n_heads = c // LANES
    body = functools.partial(_diff_attn_body, tk, lam_init)
    qspec = pl.BlockSpec((1, tq, LANES), lambda bi, h, i: (bi, i, h))
    kvspec = pl.BlockSpec((1, s, LANES), lambda bi, h, i: (bi, 0, h))
    sg = jnp.broadcast_to(subln_g[:, None], (LANES, tq))
    return pl.pallas_call(
        body,
        grid=(b, n_heads, s // tq),
        in_specs=[_resident(lam_vecs.shape), _resident((LANES, tq)), qspec, kvspec, kvspec],
        out_specs=qspec,
        out_shape=jax.ShapeDtypeStruct((b, s, c), BF16),
        scratch_shapes=[pltpu.VMEM((s // tk, LANES + BF16_ROWS, tk), BF16),
                        pltpu.VMEM((LANES, 2 * tq), BF16),
                        pltpu.VMEM((depth, tk, 2 * tq), F32),
                        pltpu.VMEM((depth, tk, 2 * tq), BF16),
                        pltpu.VMEM((LANES + BF16_ROWS, 2 * tq), F32)],
        compiler_params=_cparams(("parallel", "parallel", "arbitrary")),
    )(lam_vecs, sg, q, k, v)


def _rope_tables(seq, dim, scale):
    inv = ROPE_THETA ** (-jnp.arange(0, dim, 2, dtype=F32) / dim)
    ang = jnp.arange(seq, dtype=F32)[:, None] * inv[None, :]
    reps = (LANES // 2) // (dim // 2)
    cos = jnp.tile(jnp.cos(ang), (1, 2 * reps))
    sin = jnp.tile(jnp.sin(ang), (1, reps))
    sin = jnp.concatenate([-sin, sin], axis=1)
    return cos, sin, cos * scale, sin * scale


def _pair_interleave(a0, b0, half_dim):
    r = np.arange(half_dim)
    return np.concatenate([a0 + r, b0 + r, a0 + half_dim + r, b0 + half_dim + r])


def _cast_w(w, cols=None):
    if cols is not None:
        w = w[:, cols]
    return w.astype(BF16)


def _conv_layer(x, shift, scale, gate, g, w_in, conv_k, w_out, tm):
    b, s, d = x.shape
    e = w_out.shape[0]
    ops = [("mul", (e, 2 * e), e, [(0, 0, 1)], None),
           ("mulsilu", (0, 3 * e), e, [(1, 0, 1)], None)]
    t, gz = _inproj(x, shift, scale, g, _cast_w(w_in), [], [(e, 1), (e, 1)], ops, tm)
    halo = 16
    hb = tm // halo
    last = s // halo - 1
    specs = [
        pl.BlockSpec((1, halo, e), lambda bi, i: (bi, jnp.maximum(i * hb - 1, 0), 0)),
        _tile_spec(tm, e),
        pl.BlockSpec((1, halo, e), lambda bi, i: (bi, jnp.minimum((i + 1) * hb, last), 0)),
        _tile_spec(tm, e),
        _resident(conv_k.shape),
    ] + _outproj_common_specs(tm, d)
    body = functools.partial(_outproj_conv_body, halo)
    return _outproj_call(body, specs, (t, t, t, gz, conv_k, _cast_w(w_out), x, gate), x, tm,
                         scratch=[pltpu.VMEM((tm + 2 * halo, e), F32)])


def _dil_layer(x, shift, scale, gate, g, w_in, w_out, tm):
    b, s, d = x.shape
    n_grp = len(DIL_RATES)
    e = w_out.shape[0]
    dh = e // DIL_HEADS
    cos, sin, cos_q, sin_q = _rope_tables(s, dh, dh ** -0.5)
    outs = [(e, r) for r in DIL_RATES] * 3 + [(e, 1)]
    ops = []
    for gi, r in enumerate(DIL_RATES):
        ops.append(("rope", (gi * e,), e, [(gi, 0, r)], (2, 3)))
        ops.append(("rope", ((n_grp + gi) * e,), e, [(n_grp + gi, 0, r)], (0, 1)))
    ops.append(("plain", (2 * n_grp * e,), e,
                [(2 * n_grp + gi, 0, r) for gi, r in enumerate(DIL_RATES)], None))
    ops.append(("silu", (2 * n_grp * e + e,), e, [(3 * n_grp, 0, 1)], None))
    res = _inproj(x, shift, scale, g, _cast_w(w_in), [cos, sin, cos_q, sin_q], outs, ops, tm)
    zs = res[3 * n_grp]
    os_, lses = [], []
    for gi, rate in enumerate(DIL_RATES):
        half = DIL_WINDOWS[gi] // (2 * rate)
        grouped = lambda t: t.reshape(b, rate, s // rate, e)
        o, lse = _dil_attention(grouped(res[gi]), grouped(res[n_grp + gi]),
                                grouped(res[2 * n_grp + gi]), half, 512)
        os_.append(o)
        lses.append(lse)
    rows = np.arange(2 * LANES)[:, None] % LANES
    expand = jnp.asarray(rows == (np.arange(e)[None, :] // dh), dtype=BF16)
    grouped_spec = lambda r, wd: pl.BlockSpec((1, r, tm // r, wd), lambda bi, i: (bi, 0, i, 0))
    specs = ([grouped_spec(r, e) for r in DIL_RATES] + [grouped_spec(r, LANES) for r in DIL_RATES]
             + [_tile_spec(tm, e), _resident(expand.shape)] + _outproj_common_specs(tm, d))
    body = functools.partial(_outproj_dil_body, n_grp)
    scratch = ([pltpu.VMEM((e // LANES, tm, LANES), F32)] * n_grp
               + [pltpu.VMEM((1, tm, LANES), F32)] * n_grp)
    return _outproj_call(body, specs, (*os_, *lses, zs, expand, _cast_w(w_out), x, gate), x, tm,
                         scratch=scratch)


def _swa_layer(x, shift, scale, gate, g, w_in, sink, w_out, tm):
    b, s, d = x.shape
    hq, hk, dh = SWA_Q_HEADS, SWA_KV_HEADS, SWA_HEAD_DIM
    grp = hq // hk
    pairs = [((2 * p) * grp + gi, (2 * p + 1) * grp + gi) for p in range(hk // 2) for gi in range(grp)]
    q_cols = np.concatenate([_pair_interleave(a * dh, b_ * dh, dh // 2) for a, b_ in pairs])
    k_cols = hq * dh + np.concatenate(
        [_pair_interleave(2 * p * dh, (2 * p + 1) * dh, dh // 2) for p in range(hk // 2)])
    v_cols = hq * dh + hk * dh + np.arange(hk * dh)
    head_cols = np.concatenate([np.concatenate([a * dh + np.arange(dh), b_ * dh + np.arange(dh)])
                                for a, b_ in pairs])
    z_cols = hq * dh + 2 * hk * dh + head_cols
    w = _cast_w(w_in, np.concatenate([q_cols, k_cols, v_cols, z_cols]))
    sink_pairs = sink[np.array([h for pr in pairs for h in pr])] * math.log2(math.e)
    cos, sin, cos_q, sin_q = _rope_tables(s, dh, dh ** -0.5 * math.log2(math.e))
    nq, nk = hq * dh, hk * dh
    ops = [("rope", (0,), nq, [(0, 0, 1)], (2, 3)),
           ("rope", (nq,), nk, [(1, 0, 1)], (0, 1)),
           ("plain", (nq + nk,), nk, [(2, 0, 1)], None),
           ("silu", (nq + 2 * nk,), nq, [(3, 0, 1)], None)]
    q, k, v, zs = _inproj(x, shift, scale, g, w, [cos, sin, cos_q, sin_q],
                          [(nq, 1), (nk, 1), (nk, 1), (nq, 1)], ops, tm)
    o = _swa_attention(q, k, v, sink_pairs, SWA_HALF, 512, grp)
    specs = [_tile_spec(tm, nq), _tile_spec(tm, nq)] + _outproj_common_specs(tm, d)
    body = functools.partial(_outproj_mul_body, False)
    return _outproj_call(body, specs, (o, zs, _cast_w(w_out[head_cols]), x, gate), x, tm)


def _diff_layer(x, shift, scale, gate, g, w_in, lam_vecs, subln_g, w_out, final_g, layer_idx, tm):
    b, s, d = x.shape
    nh, dh = DIFF_HEADS, DIFF_HEAD_DIM
    e = nh * 2 * dh
    qk_cols = np.concatenate([_pair_interleave(h * 2 * dh, h * 2 * dh + dh, dh // 2) for h in range(nh)])
    cols = np.concatenate([qk_cols, e + qk_cols, 2 * e + np.arange(2 * e)])
    w = _cast_w(w_in, cols)
    cos, sin, cos_q, sin_q = _rope_tables(s, dh, dh ** -0.5 * math.log2(math.e))
    ops = [("rope", (0,), e, [(0, 0, 1)], (2, 3)),
           ("rope", (e,), e, [(1, 0, 1)], (0, 1)),
           ("plain", (2 * e,), e, [(2, 0, 1)], None),
           ("silu", (3 * e,), e, [(3, 0, 1)], None)]
    q, k, v, zs = _inproj(x, shift, scale, g, w, [cos, sin, cos_q, sin_q], [(e, 1)] * 4, ops, tm)
    lam_init = 0.8 - 0.6 * math.exp(-0.3 * layer_idx)
    o = _diff_attention(q, k, v, lam_vecs, subln_g, lam_init, 512, 256)
    specs = ([_tile_spec(tm, e), _tile_spec(tm, e)] + _outproj_common_specs(tm, d)
             + ([_resident((1, d))] if final_g is not None else []))
    body = functools.partial(_outproj_mul_body, final_g is not None)
    args = (o, zs, _cast_w(w_out), x, gate) + ((final_g.reshape(1, d),) if final_g is not None else ())
    return _outproj_call(body, specs, args, x, tm)


def kernel(x, c, norm_g, w_mod, b_mod, conv_w_in, conv_k, conv_w_out, dil_w_in, dil_w_out,
           swa_w_in, swa_sink, swa_w_out, diff_w_in, diff_lambda, diff_subln_g, diff_w_out,
           final_g):
    b, s, d = x.shape
    depth = norm_g.shape[0]
    tm = min(512, s)
    mod = _modulation(c, w_mod, b_mod)
    for i in range(depth):
        kind, j = i % N_MIXERS, i // N_MIXERS
        shift, scale, gate = (mod[i, t][:, None, :] for t in range(3))
        g = norm_g[i].reshape(1, d)
        last = i == depth - 1
        if kind == 0:
            x = _conv_layer(x, shift, scale, gate, g, conv_w_in[j], conv_k[j], conv_w_out[j], tm)
        elif kind == 1:
            x = _dil_layer(x, shift, scale, gate, g, dil_w_in[j], dil_w_out[j], tm)
        elif kind == 2:
            x = _swa_layer(x, shift, scale, gate, g, swa_w_in[j], swa_sink[j], swa_w_out[j], tm)
        else:
            x = _diff_layer(x, shift, scale, gate, g, diff_w_in[j], diff_lambda[j], diff_subln_g[j],
                            diff_w_out[j], final_g if last else None, i, tm)
        if last and kind != 3:
            raise NotImplementedError("final RMSNorm is fused into the differential layer")
    return x
```

```python
import functools
import math
from typing import Callable, NamedTuple

import numpy as np
import jax
import jax.numpy as jnp
from jax import lax
from jax.experimental import pallas as pl
from jax.experimental.pallas import tpu as pltpu

F32 = jnp.float32
BF16 = jnp.bfloat16

LANES = 128
BF16_ROWS = 16
VMEM_LIMIT = 56 * 1024 * 1024

N_MIXERS = 4
CONV_WIDTH = 3
DIL_WINDOWS = (128, 512, 2048)
DIL_RATES = (1, 4, 16)
DIL_HEADS = 8
SWA_HALF = 128
SWA_Q_HEADS = 16
SWA_KV_HEADS = 4
SWA_HEAD_DIM = 64
DIFF_HEADS = 8
DIFF_HEAD_DIM = 64
ROPE_THETA = 10000.0
NORM_EPS = 1e-6
SUBLN_EPS = 1e-5
NEG_INF = -1e30


def _cparams(sem):
    return pltpu.CompilerParams(dimension_semantics=sem, vmem_limit_bytes=VMEM_LIMIT)


def _resident(shape):
    nd = len(shape)
    return pl.BlockSpec(shape, lambda *_: (0,) * nd)


def _mod_body(c_ref, w_ref, b_ref, o_ref):
    c = c_ref[...]
    sc = c * jax.nn.sigmoid(c)
    o_ref[0, 0] = jnp.dot(sc, w_ref[0], precision=lax.Precision.HIGHEST,
                          preferred_element_type=F32) + b_ref[0, 0]


def _modulation(c, w_mod, b_mod):
    depth, d, _ = w_mod.shape
    b = c.shape[0]
    rows = -(-b // 8) * 8
    cp = jnp.pad(c, ((0, rows - b), (0, 0)))
    out = pl.pallas_call(
        _mod_body,
        grid=(depth, 3),
        in_specs=[
            pl.BlockSpec((rows, d), lambda i, j: (0, 0)),
            pl.BlockSpec((1, d, d), lambda i, j: (i, 0, j)),
            pl.BlockSpec((1, 1, 1, d), lambda i, j: (i, j, 0, 0)),
        ],
        out_specs=pl.BlockSpec((1, 1, rows, d), lambda i, j: (i, j, 0, 0)),
        out_shape=jax.ShapeDtypeStruct((depth, 3, rows, d), F32),
        compiler_params=_cparams(("arbitrary", "arbitrary")),
    )(cp, w_mod, b_mod.reshape(depth, 3, 1, d))
    return out[:, :, :b]


def _silu(a):
    return a * jax.nn.sigmoid(a)


def _inproj_body(ops, row_rates, n_tab, n_out, chunk, prev_fn, n_x, *refs):
    shift_ref, scale_ref, g_ref, w_ref = refs[n_x:n_x + 4]
    rest = refs[n_x + 4:]
    tabs = rest[:n_tab]
    outs = rest[n_tab:n_tab + n_out]
    n_own = 2 + (1 + len(row_rates) if row_rates else 0)
    h_ref, tmp_ref = rest[n_tab + n_out:n_tab + n_out + 2]
    h_by_rate = {1: h_ref}
    if prev_fn is None:
        x = refs[0][0]
    else:
        x = prev_fn(refs[:n_x], rest[n_tab + n_out + n_own:])
        outs[0][0] = x
        outs = outs[1:]
    tm, d = x.shape
    ms = jnp.mean(x * x, axis=-1, keepdims=True)
    y = x * lax.rsqrt(ms + NORM_EPS) * g_ref[...]
    h = y * (1.0 + scale_ref[0]) + shift_ref[0]
    h_ref[...] = h.astype(BF16)
    if row_rates:
        stage_ref = rest[n_tab + n_out + 2]
        for j in range(d // LANES):
            stage_ref[j] = h[:, j * LANES:(j + 1) * LANES]
        for rate, hp_ref in zip(row_rates, rest[n_tab + n_out + 3:]):
            h_by_rate[rate] = hp_ref
            n_rows = tm // rate
            for p in range(rate):
                for j in range(d // LANES):
                    hp_ref[p * n_rows:(p + 1) * n_rows, j * LANES:(j + 1) * LANES] = (
                        stage_ref[j, pl.ds(p, n_rows, stride=rate), :].astype(BF16))

    for kind, srcs, ncols, dests, tab, row_rate in ops:
        src_ref = h_by_rate[row_rate]

        def proj(c0, n, src_ref=src_ref):
            return jnp.dot(src_ref[...], w_ref[:, c0:c0 + n], preferred_element_type=F32)

        for c in range(0, ncols, chunk):
            n = min(chunk, ncols - c)
            if kind == "plain":
                r = proj(srcs[0] + c, n)
            elif kind == "silu":
                r = _silu(proj(srcs[0] + c, n))
            elif kind == "mul":
                r = proj(srcs[0] + c, n) * proj(srcs[1] + c, n)
            elif kind == "mulsilu":
                r = proj(srcs[0] + c, n) * _silu(proj(srcs[1] + c, n))
            elif kind == "rope":
                a = proj(srcs[0] + c, n)
                cos = tabs[tab[0]][...].reshape(tm, LANES)
                sin = tabs[tab[1]][...].reshape(tm, LANES)
                parts = []
                for j in range(n // LANES):
                    aj = a[:, j * LANES:(j + 1) * LANES]
                    parts.append(aj * cos + pltpu.roll(aj, LANES // 2, 1) * sin)
                r = parts[0] if len(parts) == 1 else jnp.concatenate(parts, axis=1)
            else:
                raise ValueError(kind)
            staged = False
            for oi, ooff, rate in dests:
                o_ref = outs[oi]
                cols = slice(ooff + c, ooff + c + n)
                if rate == 1:
                    o_ref[0, :, cols] = r.astype(o_ref.dtype)
                    continue
                if rate == row_rate:
                    n_rows = tm // rate
                    for p in range(rate):
                        o_ref[0, p, :, cols] = r[p * n_rows:(p + 1) * n_rows].astype(o_ref.dtype)
                    continue
                assert row_rate == 1
                if not staged:
                    for j in range(n // LANES):
                        tmp_ref[j] = r[:, j * LANES:(j + 1) * LANES]
                    staged = True
                for p in range(rate):
                    for j in range(n // LANES):
                        o_ref[0, p, :, ooff + c + j * LANES:ooff + c + (j + 1) * LANES] = (
                            tmp_ref[j, pl.ds(p, tm // rate, stride=rate), :].astype(o_ref.dtype))


def _inproj(x, shift, scale, g, w, tables, outs, ops, tm, chunk=512):
    fused = isinstance(x, _Residual)
    b, s, d = x.shape
    n = w.shape[1]
    x_specs, x_args = (list(x.specs), tuple(x.args)) if fused else ([_tile_spec(tm, d)], (x,))
    in_specs = x_specs + [
        pl.BlockSpec((1, 1, d), lambda bi, i: (bi, 0, 0)),
        pl.BlockSpec((1, 1, d), lambda bi, i: (bi, 0, 0)),
        _resident((1, d)),
        pl.BlockSpec((d, n), lambda bi, i: (0, 0), pipeline_mode=pl.Buffered(1)),
    ]
    for t in tables:
        if t.ndim == 2:
            in_specs.append(pl.BlockSpec((tm, LANES), lambda bi, i: (i, 0)))
        else:
            in_specs.append(pl.BlockSpec((t.shape[0], tm // t.shape[0], LANES),
                                         lambda bi, i: (0, i, 0)))
    row_rates = tuple(sorted({op[5] for op in ops} - {1}))
    scratch = [pltpu.VMEM((tm, d), BF16), pltpu.VMEM((chunk // LANES, tm, LANES), F32)]
    if row_rates:
        scratch += [pltpu.VMEM((d // LANES, tm, LANES), F32)] + [pltpu.VMEM((tm, d), BF16)] * len(row_rates)
    out_specs, out_shape = [], []
    if fused:
        scratch += list(x.scratch)
        out_specs.append(_tile_spec(tm, d))
        out_shape.append(jax.ShapeDtypeStruct((b, s, d), F32))
    for wd, rate in outs:
        if rate == 1:
            out_specs.append(pl.BlockSpec((1, tm, wd), lambda bi, i: (bi, i, 0)))
            out_shape.append(jax.ShapeDtypeStruct((b, s, wd), BF16))
        else:
            out_specs.append(pl.BlockSpec((1, rate, tm // rate, wd), lambda bi, i: (bi, 0, i, 0)))
            out_shape.append(jax.ShapeDtypeStruct((b, rate, s // rate, wd), BF16))
    body = functools.partial(_inproj_body, tuple(ops), row_rates, len(tables), len(out_specs), chunk,
                             x.fn if fused else None, len(x_specs))
    res = pl.pallas_call(
        body,
        grid=(b, s // tm),
        in_specs=in_specs,
        out_specs=out_specs,
        out_shape=out_shape,
        scratch_shapes=scratch,
        compiler_params=_cparams(("parallel", "parallel")),
    )(*x_args, shift, scale, g, w, *tables)
    return (res[0], res[1:]) if fused else (x, res)


def _outproj_tail(y, w_ref, x_ref, gate_ref):
    acc = jnp.dot(y.astype(BF16), w_ref[...], preferred_element_type=F32)
    return x_ref[0] + gate_ref[0] * acc


def _conv_residual(halo, in_refs, scratch_refs):
    t_prev_ref, t_ref, t_next_ref, g_ref, ck_ref, w_ref, x_ref, gate_ref = in_refs
    win_ref, = scratch_refs
    i = pl.program_id(1)
    tm = t_ref.shape[1]
    has_prev = (i > 0).astype(F32)
    has_next = (i < pl.num_programs(1) - 1).astype(F32)
    win_ref[0:halo] = t_prev_ref[0].astype(F32) * has_prev
    win_ref[halo:halo + tm] = t_ref[0].astype(F32)
    win_ref[halo + tm:2 * halo + tm] = t_next_ref[0].astype(F32) * has_next
    ck = ck_ref[...]
    conv = (win_ref[halo - 1:halo - 1 + tm] * ck[0:1]
            + win_ref[halo:halo + tm] * ck[1:2]
            + win_ref[halo + 1:halo + 1 + tm] * ck[2:3])
    y = g_ref[0].astype(F32) * conv
    return _outproj_tail(y, w_ref, x_ref, gate_ref)


def _interleave(src_ref, stage_ref):
    rate = src_ref.shape[1]
    if rate == 1:
        return src_ref[0, 0].astype(F32)
    n = src_ref.shape[2]
    n_col = src_ref.shape[3] // LANES
    for p in range(rate):
        for j in range(n_col):
            stage_ref[j, pl.ds(p, n, stride=rate), :] = (
                src_ref[0, p, :, j * LANES:(j + 1) * LANES].astype(F32))
    if n_col == 1:
        return stage_ref[0]
    return jnp.concatenate([stage_ref[j] for j in range(n_col)], axis=1)


def _dil_residual(n_grp, in_refs, scratch_refs):
    o_refs = in_refs[:n_grp]
    lse_refs = in_refs[n_grp:2 * n_grp]
    zs_ref, expand_ref, w_ref, x_ref, gate_ref = in_refs[2 * n_grp:]
    o_stage = scratch_refs[:n_grp]
    lse_stage = scratch_refs[n_grp:]
    lses = [_interleave(r, st) for r, st in zip(lse_refs, lse_stage)]
    mx = functools.reduce(jnp.maximum, lses)
    es = [jnp.exp(l - mx) for l in lses]
    inv = 1.0 / functools.reduce(jnp.add, es)
    o = None
    for gi in range(n_grp):
        wt = es[gi] * inv
        hi = wt.astype(BF16)
        lo = (wt - hi.astype(F32)).astype(BF16)
        wfull = jnp.dot(jnp.concatenate([hi, lo], axis=1), expand_ref[...],
                        preferred_element_type=F32)
        term = wfull * _interleave(o_refs[gi], o_stage[gi])
        o = term if o is None else o + term
    y = o * zs_ref[0].astype(F32)
    return _outproj_tail(y, w_ref, x_ref, gate_ref)


def _mul_residual(in_refs, scratch_refs):
    o_ref, zs_ref, w_ref, x_ref, gate_ref = in_refs
    del scratch_refs
    y = o_ref[0].astype(F32) * zs_ref[0].astype(F32)
    return _outproj_tail(y, w_ref, x_ref, gate_ref)


class _Residual(NamedTuple):
    fn: Callable
    specs: list
    args: tuple
    scratch: list
    shape: tuple


def _outproj_body(fn, n_in, final, *refs):
    xn = fn(refs[:n_in], refs[n_in + final + 1:])
    if final:
        ms = jnp.mean(xn * xn, axis=-1, keepdims=True)
        xn = xn * lax.rsqrt(ms + NORM_EPS) * refs[n_in][...]
    refs[n_in + final][0] = xn


def _tile_spec(tm, d):
    return pl.BlockSpec((1, tm, d), lambda bi, i: (bi, i, 0))


def _outproj_common_specs(tm, d):
    return [
        _resident((d, d)),
        _tile_spec(tm, d),
        pl.BlockSpec((1, 1, d), lambda bi, i: (bi, 0, 0)),
    ]


def _outproj_call(res, tm, final_g=None):
    b, s, d = res.shape
    final = final_g is not None
    specs = list(res.specs) + ([_resident((1, d))] if final else [])
    args = tuple(res.args) + ((final_g.reshape(1, d),) if final else ())
    return pl.pallas_call(
        functools.partial(_outproj_body, res.fn, len(res.specs), int(final)),
        grid=(b, s // tm),
        in_specs=specs,
        out_specs=_tile_spec(tm, d),
        out_shape=jax.ShapeDtypeStruct((b, s, d), F32),
        scratch_shapes=list(res.scratch),
        compiler_params=_cparams(("parallel", "parallel")),
    )(*args)


def _fill_window(win_ref, prev_ref, cur_ref, next_ref, half, tq_blk):
    win_ref[0:half] = prev_ref[0]
    win_ref[half:half + tq_blk] = cur_ref[0]
    win_ref[half + tq_blk:2 * half + tq_blk] = next_ref[0]


def _band_mask(half, tq, j, n_sub, blk, n_blk):
    nk = tq + 2 * half
    row = lax.broadcasted_iota(jnp.int32, (tq, nk), 0)
    col = lax.broadcasted_iota(jnp.int32, (tq, nk), 1)
    delta = col - row
    mask = (delta >= 0) & (delta <= 2 * half)
    if j == 0:
        mask = mask & ((col >= half) | (blk > 0))
    if j == n_sub - 1:
        mask = mask & ((col < tq + half) | (blk < n_blk - 1))
    return mask


def _softmax_pv(s, mask, v, sink=None):
    s = jnp.where(mask, s, NEG_INF)
    m = jnp.max(s, axis=-1, keepdims=True)
    if sink is not None:
        m = jnp.maximum(m, sink)
    p = jnp.exp(s - m)
    l = jnp.sum(p, axis=-1, keepdims=True)
    if sink is not None:
        l = l + jnp.exp(sink - m)
    o = jnp.dot(p.astype(BF16), v, preferred_element_type=F32)
    return o / l, m, l


def _qk(q, k):
    return lax.dot_general(q, k, (((1,), (1,)), ((), ())), preferred_element_type=F32)


def _dil_attn_body(half, n_heads, q_ref, kp_ref, kc_ref, kn_ref, vp_ref, vc_ref, vn_ref,
                   o_ref, lse_ref, kwin_ref, vwin_ref):
    blk = pl.program_id(2)
    n_blk = pl.num_programs(2)
    tq_blk = q_ref.shape[1]
    tq = 2 * half
    n_sub = tq_blk // tq
    _fill_window(kwin_ref, kp_ref, kc_ref, kn_ref, half, tq_blk)
    _fill_window(vwin_ref, vp_ref, vc_ref, vn_ref, half, tq_blk)
    lane = lax.broadcasted_iota(jnp.int32, (tq, LANES), 1)
    for j in range(n_sub):
        mask = _band_mask(half, tq, j, n_sub, blk, n_blk)
        r0 = j * tq
        lse_tile = jnp.zeros((tq, LANES), F32)
        for h in range(n_heads):
            cs = slice(h * LANES, (h + 1) * LANES)
            s = _qk(q_ref[0, r0:r0 + tq, cs], kwin_ref[r0:r0 + tq + 2 * half, cs])
            o, m, l = _softmax_pv(s, mask, vwin_ref[r0:r0 + tq + 2 * half, cs])
            o_ref[0, r0:r0 + tq, cs] = o.astype(o_ref.dtype)
            lse_tile = jnp.where(lane == h, m + jnp.log(l), lse_tile)
        lse_ref[0, r0:r0 + tq, :] = lse_tile


def _dil_attention(q, k, v, half, tq_blk):
    b, rate, l, c = v.shape
    tq_blk = min(tq_blk, l)
    n_blk = l // tq_blk
    hb = tq_blk // half
    last_halo = l // half - 1
    cur = pl.BlockSpec((1, None, tq_blk, c), lambda bi, p, i: (bi, p, i, 0))
    prev = pl.BlockSpec((1, None, half, c), lambda bi, p, i: (bi, p, jnp.maximum(i * hb - 1, 0), 0))
    nxt = pl.BlockSpec((1, None, half, c),
                       lambda bi, p, i: (bi, p, jnp.minimum((i + 1) * hb, last_halo), 0))
    body = functools.partial(_dil_attn_body, half, c // LANES)
    return pl.pallas_call(
        body,
        grid=(b, rate, n_blk),
        in_specs=[cur, prev, cur, nxt, prev, cur, nxt],
        out_specs=[cur, pl.BlockSpec((1, None, tq_blk, LANES), lambda bi, p, i: (bi, p, i, 0))],
        out_shape=[jax.ShapeDtypeStruct((b, rate, l, c), BF16),
                   jax.ShapeDtypeStruct((b, rate, l, LANES), F32)],
        scratch_shapes=[pltpu.VMEM((tq_blk + 2 * half, c), BF16),
                        pltpu.VMEM((tq_blk + 2 * half, c), BF16)],
        compiler_params=_cparams(("parallel", "parallel", "parallel")),
    )(q, k, k, k, v, v, v)


def _swa_attn_body(half, n_pairs, q_per_kv, sink_ref, q_ref, kp_ref, kc_ref, kn_ref,
                   vp_ref, vc_ref, vn_ref, o_ref, kwin_ref, vwin_ref, s_ref):
    blk = pl.program_id(1)
    n_blk = pl.num_programs(1)
    tq_blk = q_ref.shape[1]
    tq = half
    nk = tq + 2 * half
    n_sub = tq_blk // tq
    n_stack = 2 * q_per_kv
    _fill_window(kwin_ref, kp_ref, kc_ref, kn_ref, half, tq_blk)
    for p in range(n_pairs):
        vs = slice(p * LANES, (p + 1) * LANES)
        ws = slice(2 * p * LANES, (2 * p + 1) * LANES)
        vwin_ref[0:half, ws] = vp_ref[0, :, vs]
        vwin_ref[half:half + tq_blk, ws] = vc_ref[0, :, vs]
        vwin_ref[half + tq_blk:2 * half + tq_blk, ws] = vn_ref[0, :, vs]
        vwin_ref[:, (2 * p + 1) * LANES:(2 * p + 2) * LANES] = jnp.ones(
            (tq_blk + 2 * half, LANES), BF16)
    lane = lax.broadcasted_iota(jnp.int32, (tq, LANES), 1)
    first = (lane % (LANES // 2)) < (LANES // 4)
    units = [(j, p) for j in range(n_sub) for p in range(n_pairs)]

    def scores(i):
        j, p = units[i]
        r0 = j * tq
        qs = []
        for g in range(q_per_kv):
            u = p * q_per_kv + g
            q = q_ref[0, r0:r0 + tq, u * LANES:(u + 1) * LANES]
            zero = jnp.zeros_like(q)
            qs += [jnp.where(first, q, zero), jnp.where(first, zero, q)]
        s_ref[i % 2] = _qk(jnp.concatenate(qs, axis=0),
                           kwin_ref[r0:r0 + nk, p * LANES:(p + 1) * LANES])

    scores(0)
    for i, (j, p) in enumerate(units):
        if i + 1 < len(units):
            scores(i + 1)
        r0 = j * tq
        mask = _band_mask(half, tq, j, n_sub, blk, n_blk)[None]
        sinks = []
        for g in range(q_per_kv):
            u = p * q_per_kv + g
            sinks += [jnp.full((tq, 1), sink_ref[2 * u], F32),
                      jnp.full((tq, 1), sink_ref[2 * u + 1], F32)]
        sink = jnp.concatenate(sinks, axis=0)
        s = jnp.where(mask, s_ref[i % 2].reshape(n_stack, tq, nk), NEG_INF).reshape(n_stack * tq, nk)
        m = jnp.maximum(jnp.max(s, axis=-1, keepdims=True), sink)
        pr = jnp.exp2((s - m).astype(BF16))
        out = jnp.dot(pr, vwin_ref[r0:r0 + nk, 2 * p * LANES:(2 * p + 2) * LANES],
                      preferred_element_type=F32)
        o = out[:, 0:LANES] / (out[:, LANES:2 * LANES] + jnp.exp2(sink - m))
        for g in range(q_per_kv):
            u = p * q_per_kv + g
            oa = o[2 * g * tq:(2 * g + 1) * tq]
            ob = o[(2 * g + 1) * tq:(2 * g + 2) * tq]
            o_ref[0, r0:r0 + tq, u * LANES:(u + 1) * LANES] = jnp.where(
                lane < LANES // 2, oa, ob).astype(o_ref.dtype)


def _swa_attention(q, k, v, sink_pairs, half, tq_blk, q_per_kv):
    b, s, cq = q.shape
    ck = k.shape[-1]
    tq_blk = min(tq_blk, s)
    n_blk = s // tq_blk
    hb = tq_blk // half
    last_halo = s // half - 1
    qspec = pl.BlockSpec((1, tq_blk, cq), lambda bi, i: (bi, i, 0))
    cur = pl.BlockSpec((1, tq_blk, ck), lambda bi, i: (bi, i, 0))
    prev = pl.BlockSpec((1, half, ck), lambda bi, i: (bi, jnp.maximum(i * hb - 1, 0), 0))
    nxt = pl.BlockSpec((1, half, ck), lambda bi, i: (bi, jnp.minimum((i + 1) * hb, last_halo), 0))
    body = functools.partial(_swa_attn_body, half, ck // LANES, q_per_kv)
    return pl.pallas_call(
        body,
        grid=(b, n_blk),
        in_specs=[pl.BlockSpec(memory_space=pltpu.SMEM), qspec, prev, cur, nxt, prev, cur, nxt],
        out_specs=qspec,
        out_shape=jax.ShapeDtypeStruct((b, s, cq), BF16),
        scratch_shapes=[pltpu.VMEM((tq_blk + 2 * half, ck), BF16),
                        pltpu.VMEM((tq_blk + 2 * half, 2 * ck), BF16),
                        pltpu.VMEM((2, 2 * q_per_kv * half, 3 * half), F32)],
        compiler_params=_cparams(("parallel", "parallel")),
    )(sink_pairs, q, k, k, k, v, v, v)


def _diff_attn_body(tk, lam_init, lam_ref, sg_ref, q_ref, k_ref, v_ref, o_ref,
                    vt_ref, qt_ref, st_ref, pt_ref, acc_ref):
    tq = q_ref.shape[1]
    n_kv = k_ref.shape[1] // tk

    @pl.when(pl.program_id(2) == 0)
    def _():
        for j in range(n_kv):
            vt_ref[j, 0:LANES] = v_ref[0, j * tk:(j + 1) * tk, :].astype(F32).T.astype(BF16)
            vt_ref[j, LANES:] = jnp.ones((vt_ref.shape[1] - LANES, tk), BF16)

    row = lax.broadcasted_iota(jnp.int32, (LANES, tq), 0)
    first = (row % (LANES // 2)) < (LANES // 4)
    qt = q_ref[0].astype(F32).T
    qt_ref[:, 0:tq] = jnp.where(first, qt, 0.0).astype(BF16)
    qt_ref[:, tq:2 * tq] = jnp.where(first, 0.0, qt).astype(BF16)
    acc_ref[...] = jnp.zeros(acc_ref.shape, F32)

    def fold(x, op):
        while x.shape[0] > 8:
            half = x.shape[0] // 2
            x = op(x[:half], x[half:])
        return x

    def scores(kk, slot):
        k0 = kk * tk if isinstance(kk, int) else pl.multiple_of(kk * tk, tk)
        st_ref[slot] = jnp.dot(k_ref[0, pl.ds(k0, tk), :], qt_ref[...],
                               preferred_element_type=F32)

    def softmax(slot, m_old):
        st = st_ref[slot]
        m_new = jnp.maximum(m_old, jnp.max(fold(st, jnp.maximum), axis=0, keepdims=True))
        pt_ref[slot] = jnp.exp2((st - m_new).astype(BF16))
        return m_new, jnp.exp2(m_old - m_new)

    depth = st_ref.shape[0]

    def tile_step(kk, slot, state, more_scores=True):
        m, alpha = state
        pv = jnp.dot(vt_ref[kk], pt_ref[slot], preferred_element_type=F32)
        if more_scores:
            scores(kk + depth, slot)
        state = softmax((slot + 1) % depth, m)
        acc_ref[...] = alpha * acc_ref[...] + pv
        return state

    def step(i, state):
        for j in range(depth):
            state = tile_step(depth * i + j, j, state)
        return state

    for j in range(depth):
        scores(j, j)
    state = softmax(0, jnp.full((1, 2 * tq), NEG_INF, F32))
    n_loop = (n_kv - depth) // depth
    state = lax.fori_loop(0, n_loop, step, state)
    for kk in range(n_loop * depth, n_kv - 1):
        state = tile_step(kk, kk % depth, state, more_scores=kk + depth < n_kv)
    _, alpha = state
    acc_ref[...] = alpha * acc_ref[...] + jnp.dot(vt_ref[n_kv - 1], pt_ref[(n_kv - 1) % depth],
                                                  preferred_element_type=F32)

    lv = lam_ref[...]
    lam = (jnp.exp(jnp.sum(lv[0:1] * lv[1:2], axis=-1, keepdims=True))
           - jnp.exp(jnp.sum(lv[2:3] * lv[3:4], axis=-1, keepdims=True)) + lam_init)
    o_all = acc_ref[0:LANES] / acc_ref[LANES:LANES + 1]
    ot = o_all[:, 0:tq] - lam * o_all[:, tq:2 * tq]
    ms = jnp.mean(ot * ot, axis=0, keepdims=True)
    ot = ot * lax.rsqrt(ms + SUBLN_EPS) * sg_ref[...] * (1.0 - lam_init)
    o_ref[0] = ot.T.astype(o_ref.dtype)


def _diff_attention(q, k, v, lam_vecs, subln_g, lam_init, tq, tk, depth=3):
    b, s, c = q.shape
    tq = min(tq, s)
    tk = min(tk, s)
    assert s // tk > depth
    n_heads = c // LANES
    body = functools.partial(_diff_attn_body, tk, lam_init)
    qspec = pl.BlockSpec((1, tq, LANES), lambda bi, h, i: (bi, i, h))
    kvspec = pl.BlockSpec((1, s, LANES), lambda bi, h, i: (bi, 0, h))
    sg = jnp.broadcast_to(subln_g[:, None], (LANES, tq))
    return pl.pallas_call(
        body,
        grid=(b, n_heads, s // tq),
        in_specs=[_resident(lam_vecs.shape), _resident((LANES, tq)), qspec, kvspec, kvspec],
        out_specs=qspec,
        out_shape=jax.ShapeDtypeStruct((b, s, c), BF16),
        scratch_shapes=[pltpu.VMEM((s // tk, LANES + BF16_ROWS, tk), BF16),
                        pltpu.VMEM((LANES, 2 * tq), BF16),
                        pltpu.VMEM((depth, tk, 2 * tq), F32),
                        pltpu.VMEM((depth, tk, 2 * tq), BF16),
                        pltpu.VMEM((LANES + BF16_ROWS, 2 * tq), F32)],
        compiler_params=_cparams(("parallel", "parallel", "arbitrary")),
    )(lam_vecs, sg, q, k, v)


def _rope_tables(seq, dim, scale):
    inv = ROPE_THETA ** (-jnp.arange(0, dim, 2, dtype=F32) / dim)
    ang = jnp.arange(seq, dtype=F32)[:, None] * inv[None, :]
    reps = (LANES // 2) // (dim // 2)
    cos = jnp.tile(jnp.cos(ang), (1, 2 * reps))
    sin = jnp.tile(jnp.sin(ang), (1, reps))
    sin = jnp.concatenate([-sin, sin], axis=1)
    return cos, sin, cos * scale, sin * scale


def _pair_interleave(a0, b0, half_dim):
    r = np.arange(half_dim)
    return np.concatenate([a0 + r, b0 + r, a0 + half_dim + r, b0 + half_dim + r])


def _cast_w(w, cols=None):
    if cols is not None:
        w = w[:, cols]
    return w.astype(BF16)


def _conv_layer(x, shift, scale, gate, g, w_in, conv_k, w_out, tm):
    b, s, d = x.shape
    e = w_out.shape[0]
    ops = [("mul", (e, 2 * e), e, [(0, 0, 1)], None, 1),
           ("mulsilu", (0, 3 * e), e, [(1, 0, 1)], None, 1)]
    x, (t, gz) = _inproj(x, shift, scale, g, _cast_w(w_in), [], [(e, 1), (e, 1)], ops, tm)
    halo = 16
    hb = tm // halo
    last = s // halo - 1
    specs = [
        pl.BlockSpec((1, halo, e), lambda bi, i: (bi, jnp.maximum(i * hb - 1, 0), 0)),
        _tile_spec(tm, e),
        pl.BlockSpec((1, halo, e), lambda bi, i: (bi, jnp.minimum((i + 1) * hb, last), 0)),
        _tile_spec(tm, e),
        _resident(conv_k.shape),
    ] + _outproj_common_specs(tm, d)
    return _Residual(functools.partial(_conv_residual, halo), specs,
                     (t, t, t, gz, conv_k, _cast_w(w_out), x, gate),
                     [pltpu.VMEM((tm + 2 * halo, e), F32)], (b, s, d))


def _dil_layer(x, shift, scale, gate, g, w_in, w_out, tm):
    if isinstance(x, _Residual):
        x = _outproj_call(x, tm)
    b, s, d = x.shape
    n_grp = len(DIL_RATES)
    e = w_out.shape[0]
    dh = e // DIL_HEADS
    cos, sin, cos_q, sin_q = _rope_tables(s, dh, dh ** -0.5)
    outs = [(e, r) for r in DIL_RATES] * 3 + [(e, 1)]
    ops, tables = [], []
    for gi, r in enumerate(DIL_RATES):
        grouped = lambda t: t if r == 1 else t.reshape(s // r, r, LANES).transpose(1, 0, 2)
        tables += [grouped(t) for t in (cos, sin, cos_q, sin_q)]
        ops.append(("rope", (gi * e,), e, [(gi, 0, r)], (4 * gi + 2, 4 * gi + 3), r))
        ops.append(("rope", ((n_grp + gi) * e,), e, [(n_grp + gi, 0, r)], (4 * gi, 4 * gi + 1), r))
    ops.append(("plain", (2 * n_grp * e,), e,
                [(2 * n_grp + gi, 0, r) for gi, r in enumerate(DIL_RATES)], None, 1))
    ops.append(("silu", (2 * n_grp * e + e,), e, [(3 * n_grp, 0, 1)], None, 1))
    x, res = _inproj(x, shift, scale, g, _cast_w(w_in), tables, outs, ops, tm)
    zs = res[3 * n_grp]
    os_, lses = [], []
    for gi, rate in enumerate(DIL_RATES):
        half = DIL_WINDOWS[gi] // (2 * rate)
        grouped = lambda t: t.reshape(b, rate, s // rate, e)
        o, lse = _dil_attention(grouped(res[gi]), grouped(res[n_grp + gi]),
                                grouped(res[2 * n_grp + gi]), half, 512)
        os_.append(o)
        lses.append(lse)
    rows = np.arange(2 * LANES)[:, None] % LANES
    expand = jnp.asarray(rows == (np.arange(e)[None, :] // dh), dtype=BF16)
    grouped_spec = lambda r, wd: pl.BlockSpec((1, r, tm // r, wd), lambda bi, i: (bi, 0, i, 0))
    specs = ([grouped_spec(r, e) for r in DIL_RATES] + [grouped_spec(r, LANES) for r in DIL_RATES]
             + [_tile_spec(tm, e), _resident(expand.shape)] + _outproj_common_specs(tm, d))
    scratch = ([pltpu.VMEM((e // LANES, tm, LANES), F32)] * n_grp
               + [pltpu.VMEM((1, tm, LANES), F32)] * n_grp)
    return _Residual(functools.partial(_dil_residual, n_grp), specs,
                     (*os_, *lses, zs, expand, _cast_w(w_out), x, gate), scratch, (b, s, d))


def _swa_layer(x, shift, scale, gate, g, w_in, sink, w_out, tm):
    b, s, d = x.shape
    hq, hk, dh = SWA_Q_HEADS, SWA_KV_HEADS, SWA_HEAD_DIM
    grp = hq // hk
    pairs = [((2 * p) * grp + gi, (2 * p + 1) * grp + gi) for p in range(hk // 2) for gi in range(grp)]
    q_cols = np.concatenate([_pair_interleave(a * dh, b_ * dh, dh // 2) for a, b_ in pairs])
    k_cols = hq * dh + np.concatenate(
        [_pair_interleave(2 * p * dh, (2 * p + 1) * dh, dh // 2) for p in range(hk // 2)])
    v_cols = hq * dh + hk * dh + np.arange(hk * dh)
    head_cols = np.concatenate([np.concatenate([a * dh + np.arange(dh), b_ * dh + np.arange(dh)])
                                for a, b_ in pairs])
    z_cols = hq * dh + 2 * hk * dh + head_cols
    w = _cast_w(w_in, np.concatenate([q_cols, k_cols, v_cols, z_cols]))
    sink_pairs = sink[np.array([h for pr in pairs for h in pr])] * math.log2(math.e)
    cos, sin, cos_q, sin_q = _rope_tables(s, dh, dh ** -0.5 * math.log2(math.e))
    nq, nk = hq * dh, hk * dh
    ops = [("rope", (0,), nq, [(0, 0, 1)], (2, 3), 1),
           ("rope", (nq,), nk, [(1, 0, 1)], (0, 1), 1),
           ("plain", (nq + nk,), nk, [(2, 0, 1)], None, 1),
           ("silu", (nq + 2 * nk,), nq, [(3, 0, 1)], None, 1)]
    x, (q, k, v, zs) = _inproj(x, shift, scale, g, w, [cos, sin, cos_q, sin_q],
                               [(nq, 1), (nk, 1), (nk, 1), (nq, 1)], ops, tm)
    o = _swa_attention(q, k, v, sink_pairs, SWA_HALF, 512, grp)
    specs = [_tile_spec(tm, nq), _tile_spec(tm, nq)] + _outproj_common_specs(tm, d)
    return _Residual(_mul_residual, specs, (o, zs, _cast_w(w_out[head_cols]), x, gate), [], (b, s, d))


def _diff_layer(x, shift, scale, gate, g, w_in, lam_vecs, subln_g, w_out, layer_idx, tm):
    b, s, d = x.shape
    nh, dh = DIFF_HEADS, DIFF_HEAD_DIM
    e = nh * 2 * dh
    qk_cols = np.concatenate([_pair_interleave(h * 2 * dh, h * 2 * dh + dh, dh // 2) for h in range(nh)])
    cols = np.concatenate([qk_cols, e + qk_cols, 2 * e + np.arange(2 * e)])
    w = _cast_w(w_in, cols)
    cos, sin, cos_q, sin_q = _rope_tables(s, dh, dh ** -0.5 * math.log2(math.e))
    ops = [("rope", (0,), e, [(0, 0, 1)], (2, 3), 1),
           ("rope", (e,), e, [(1, 0, 1)], (0, 1), 1),
           ("plain", (2 * e,), e, [(2, 0, 1)], None, 1),
           ("silu", (3 * e,), e, [(3, 0, 1)], None, 1)]
    x, (q, k, v, zs) = _inproj(x, shift, scale, g, w, [cos, sin, cos_q, sin_q], [(e, 1)] * 4, ops, tm)
    lam_init = 0.8 - 0.6 * math.exp(-0.3 * layer_idx)
    o = _diff_attention(q, k, v, lam_vecs, subln_g, lam_init, 512, 256)
    specs = [_tile_spec(tm, e), _tile_spec(tm, e)] + _outproj_common_specs(tm, d)
    return _Residual(_mul_residual, specs, (o, zs, _cast_w(w_out), x, gate), [], (b, s, d))


def kernel(x, c, norm_g, w_mod, b_mod, conv_w_in, conv_k, conv_w_out, dil_w_in, dil_w_out,
           swa_w_in, swa_sink, swa_w_out, diff_w_in, diff_lambda, diff_subln_g, diff_w_out,
           final_g):
    b, s, d = x.shape
    depth = norm_g.shape[0]
    tm = min(512, s)
    mod = _modulation(c, w_mod, b_mod)
    for i in range(depth):
        kind, j = i % N_MIXERS, i // N_MIXERS
        shift, scale, gate = (mod[i, t][:, None, :] for t in range(3))
        g = norm_g[i].reshape(1, d)
        if kind == 0:
            x = _conv_layer(x, shift, scale, gate, g, conv_w_in[j], conv_k[j], conv_w_out[j], tm)
        elif kind == 1:
            x = _dil_layer(x, shift, scale, gate, g, dil_w_in[j], dil_w_out[j], tm)
        elif kind == 2:
            x = _swa_layer(x, shift, scale, gate, g, swa_w_in[j], swa_sink[j], swa_w_out[j], tm)
        else:
            x = _diff_layer(x, shift, scale, gate, g, diff_w_in[j], diff_lambda[j], diff_subln_g[j],
                            diff_w_out[j], i, tm)
    return _outproj_call(x, tm, final_g)
```

```python
import functools
import math
from typing import Callable, NamedTuple

import numpy as np
import jax
import jax.numpy as jnp
from jax import lax
from jax.experimental import pallas as pl
from jax.experimental.pallas import tpu as pltpu

F32 = jnp.float32
BF16 = jnp.bfloat16

LANES = 128
BF16_ROWS = 16
VMEM_LIMIT = 56 * 1024 * 1024

N_MIXERS = 4
CONV_WIDTH = 3
DIL_WINDOWS = (128, 512, 2048)
DIL_RATES = (1, 4, 16)
DIL_HEADS = 8
SWA_HALF = 128
SWA_Q_HEADS = 16
SWA_KV_HEADS = 4
SWA_HEAD_DIM = 64
DIFF_HEADS = 8
DIFF_HEAD_DIM = 64
ROPE_THETA = 10000.0
NORM_EPS = 1e-6
SUBLN_EPS = 1e-5
NEG_INF = -1e30


def _cparams(sem):
    return pltpu.CompilerParams(dimension_semantics=sem, vmem_limit_bytes=VMEM_LIMIT)


def _resident(shape):
    nd = len(shape)
    return pl.BlockSpec(shape, lambda *_: (0,) * nd)


def _mod_body(c_ref, w_ref, b_ref, o_ref):
    c = c_ref[...]
    sc = c * jax.nn.sigmoid(c)
    o_ref[0, 0] = jnp.dot(sc, w_ref[0], precision=lax.Precision.HIGHEST,
                          preferred_element_type=F32) + b_ref[0, 0]


def _modulation(c, w_mod, b_mod):
    depth, d, _ = w_mod.shape
    b = c.shape[0]
    rows = -(-b // 8) * 8
    cp = jnp.pad(c, ((0, rows - b), (0, 0)))
    out = pl.pallas_call(
        _mod_body,
        grid=(depth, 3),
        in_specs=[
            pl.BlockSpec((rows, d), lambda i, j: (0, 0)),
            pl.BlockSpec((1, d, d), lambda i, j: (i, 0, j)),
            pl.BlockSpec((1, 1, 1, d), lambda i, j: (i, j, 0, 0)),
        ],
        out_specs=pl.BlockSpec((1, 1, rows, d), lambda i, j: (i, j, 0, 0)),
        out_shape=jax.ShapeDtypeStruct((depth, 3, rows, d), F32),
        compiler_params=_cparams(("arbitrary", "arbitrary")),
    )(cp, w_mod, b_mod.reshape(depth, 3, 1, d))
    return out[:, :, :b]


def _silu(a):
    return a * jax.nn.sigmoid(a)


def _inproj_body(ops, row_rates, n_tab, n_out, chunk, prev_fn, n_x, *refs):
    shift_ref, scale_ref, g_ref, w_ref = refs[n_x:n_x + 4]
    rest = refs[n_x + 4:]
    tabs = rest[:n_tab]
    outs = rest[n_tab:n_tab + n_out]
    n_own = 2 + (1 + len(row_rates) if row_rates else 0)
    h_ref, tmp_ref = rest[n_tab + n_out:n_tab + n_out + 2]
    h_by_rate = {1: h_ref}
    if prev_fn is None:
        x = refs[0][0]
    else:
        x = prev_fn(refs[:n_x], rest[n_tab + n_out + n_own:])
        outs[0][0] = x
        outs = outs[1:]
    tm, d = x.shape
    ms = jnp.mean(x * x, axis=-1, keepdims=True)
    y = x * lax.rsqrt(ms + NORM_EPS) * g_ref[...]
    h = y * (1.0 + scale_ref[0]) + shift_ref[0]
    h_ref[...] = h.astype(BF16)
    if row_rates:
        stage_ref = rest[n_tab + n_out + 2]
        for j in range(d // LANES):
            stage_ref[j] = h[:, j * LANES:(j + 1) * LANES]
        for rate, hp_ref in zip(row_rates, rest[n_tab + n_out + 3:]):
            h_by_rate[rate] = hp_ref
            n_rows = tm // rate
            for p in range(rate):
                for j in range(d // LANES):
                    hp_ref[p * n_rows:(p + 1) * n_rows, j * LANES:(j + 1) * LANES] = (
                        stage_ref[j, pl.ds(p, n_rows, stride=rate), :].astype(BF16))

    for kind, srcs, ncols, dests, tab, row_rate in ops:
        src_ref = h_by_rate[row_rate]

        def proj(c0, n, src_ref=src_ref):
            return jnp.dot(src_ref[...], w_ref[:, c0:c0 + n], preferred_element_type=F32)

        for c in range(0, ncols, chunk):
            n = min(chunk, ncols - c)
            if kind == "plain":
                r = proj(srcs[0] + c, n)
            elif kind == "silu":
                r = _silu(proj(srcs[0] + c, n))
            elif kind == "mul":
                r = proj(srcs[0] + c, n) * proj(srcs[1] + c, n)
            elif kind == "mulsilu":
                r = proj(srcs[0] + c, n) * _silu(proj(srcs[1] + c, n))
            elif kind == "rope":
                a = proj(srcs[0] + c, n)
                cos = tabs[tab[0]][...].reshape(tm, LANES)
                sin = tabs[tab[1]][...].reshape(tm, LANES)
                parts = []
                for j in range(n // LANES):
                    aj = a[:, j * LANES:(j + 1) * LANES]
                    parts.append(aj * cos + pltpu.roll(aj, LANES // 2, 1) * sin)
                r = parts[0] if len(parts) == 1 else jnp.concatenate(parts, axis=1)
            else:
                raise ValueError(kind)
            staged = False
            for oi, ooff, rate in dests:
                o_ref = outs[oi]
                cols = slice(ooff + c, ooff + c + n)
                if rate == 1:
                    o_ref[0, :, cols] = r.astype(o_ref.dtype)
                    continue
                if rate == row_rate:
                    n_rows = tm // rate
                    for p in range(rate):
                        o_ref[0, p, :, cols] = r[p * n_rows:(p + 1) * n_rows].astype(o_ref.dtype)
                    continue
                assert row_rate == 1
                if not staged:
                    for j in range(n // LANES):
                        tmp_ref[j] = r[:, j * LANES:(j + 1) * LANES]
                    staged = True
                for p in range(rate):
                    for j in range(n // LANES):
                        o_ref[0, p, :, ooff + c + j * LANES:ooff + c + (j + 1) * LANES] = (
                            tmp_ref[j, pl.ds(p, tm // rate, stride=rate), :].astype(o_ref.dtype))


def _inproj(x, shift, scale, g, w, tables, outs, ops, tm, chunk=512):
    fused = isinstance(x, _Residual)
    b, s, d = x.shape
    n = w.shape[1]
    x_specs, x_args = (list(x.specs), tuple(x.args)) if fused else ([_tile_spec(tm, d)], (x,))
    in_specs = x_specs + [
        pl.BlockSpec((1, 1, d), lambda bi, i: (bi, 0, 0)),
        pl.BlockSpec((1, 1, d), lambda bi, i: (bi, 0, 0)),
        _resident((1, d)),
        pl.BlockSpec((d, n), lambda bi, i: (0, 0), pipeline_mode=pl.Buffered(1)),
    ]
    for t in tables:
        if t.ndim == 2:
            in_specs.append(pl.BlockSpec((tm, LANES), lambda bi, i: (i, 0)))
        else:
            in_specs.append(pl.BlockSpec((t.shape[0], tm // t.shape[0], LANES),
                                         lambda bi, i: (0, i, 0)))
    row_rates = tuple(sorted({op[5] for op in ops} - {1}))
    scratch = [pltpu.VMEM((tm, d), BF16), pltpu.VMEM((chunk // LANES, tm, LANES), F32)]
    if row_rates:
        scratch += [pltpu.VMEM((d // LANES, tm, LANES), F32)] + [pltpu.VMEM((tm, d), BF16)] * len(row_rates)
    out_specs, out_shape = [], []
    if fused:
        scratch += list(x.scratch)
        out_specs.append(_tile_spec(tm, d))
        out_shape.append(jax.ShapeDtypeStruct((b, s, d), F32))
    for wd, rate in outs:
        if rate == 1:
            out_specs.append(pl.BlockSpec((1, tm, wd), lambda bi, i: (bi, i, 0)))
            out_shape.append(jax.ShapeDtypeStruct((b, s, wd), BF16))
        else:
            out_specs.append(pl.BlockSpec((1, rate, tm // rate, wd), lambda bi, i: (bi, 0, i, 0)))
            out_shape.append(jax.ShapeDtypeStruct((b, rate, s // rate, wd), BF16))
    body = functools.partial(_inproj_body, tuple(ops), row_rates, len(tables), len(out_specs), chunk,
                             x.fn if fused else None, len(x_specs))
    res = pl.pallas_call(
        body,
        grid=(b, s // tm),
        in_specs=in_specs,
        out_specs=out_specs,
        out_shape=out_shape,
        scratch_shapes=scratch,
        compiler_params=_cparams(("parallel", "parallel")),
    )(*x_args, shift, scale, g, w, *tables)
    return (res[0], res[1:]) if fused else (x, res)


def _outproj_tail(y, w_ref, x_ref, gate_ref):
    acc = jnp.dot(y.astype(BF16), w_ref[...], preferred_element_type=F32)
    return x_ref[0] + gate_ref[0] * acc


def _conv_residual(halo, in_refs, scratch_refs):
    t_prev_ref, t_ref, t_next_ref, g_ref, ck_ref, w_ref, x_ref, gate_ref = in_refs
    win_ref, = scratch_refs
    i = pl.program_id(1)
    tm = t_ref.shape[1]
    has_prev = (i > 0).astype(F32)
    has_next = (i < pl.num_programs(1) - 1).astype(F32)
    win_ref[0:halo] = t_prev_ref[0].astype(F32) * has_prev
    win_ref[halo:halo + tm] = t_ref[0].astype(F32)
    win_ref[halo + tm:2 * halo + tm] = t_next_ref[0].astype(F32) * has_next
    ck = ck_ref[...]
    conv = (win_ref[halo - 1:halo - 1 + tm] * ck[0:1]
            + win_ref[halo:halo + tm] * ck[1:2]
            + win_ref[halo + 1:halo + 1 + tm] * ck[2:3])
    y = g_ref[0].astype(F32) * conv
    return _outproj_tail(y, w_ref, x_ref, gate_ref)


def _interleave(src_ref, stage_ref):
    rate = src_ref.shape[1]
    if rate == 1:
        return src_ref[0, 0].astype(F32)
    n = src_ref.shape[2]
    n_col = src_ref.shape[3] // LANES
    for p in range(rate):
        for j in range(n_col):
            stage_ref[j, pl.ds(p, n, stride=rate), :] = (
                src_ref[0, p, :, j * LANES:(j + 1) * LANES].astype(F32))
    if n_col == 1:
        return stage_ref[0]
    return jnp.concatenate([stage_ref[j] for j in range(n_col)], axis=1)


def _dil_residual(n_grp, in_refs, scratch_refs):
    o_refs = in_refs[:n_grp]
    lse_refs = in_refs[n_grp:2 * n_grp]
    zs_ref, expand_ref, w_ref, x_ref, gate_ref = in_refs[2 * n_grp:]
    o_stage = scratch_refs[:n_grp]
    lse_stage = scratch_refs[n_grp:]
    lses = [_interleave(r, st) for r, st in zip(lse_refs, lse_stage)]
    mx = functools.reduce(jnp.maximum, lses)
    es = [jnp.exp(l - mx) for l in lses]
    inv = 1.0 / functools.reduce(jnp.add, es)
    o = None
    for gi in range(n_grp):
        wt = es[gi] * inv
        hi = wt.astype(BF16)
        lo = (wt - hi.astype(F32)).astype(BF16)
        wfull = jnp.dot(jnp.concatenate([hi, lo], axis=1), expand_ref[...],
                        preferred_element_type=F32)
        term = wfull * _interleave(o_refs[gi], o_stage[gi])
        o = term if o is None else o + term
    y = o * zs_ref[0].astype(F32)
    return _outproj_tail(y, w_ref, x_ref, gate_ref)


def _mul_residual(in_refs, scratch_refs):
    o_ref, zs_ref, w_ref, x_ref, gate_ref = in_refs
    del scratch_refs
    y = o_ref[0].astype(F32) * zs_ref[0].astype(F32)
    return _outproj_tail(y, w_ref, x_ref, gate_ref)


class _Residual(NamedTuple):
    fn: Callable
    specs: list
    args: tuple
    scratch: list
    shape: tuple


def _outproj_body(fn, n_in, final, *refs):
    xn = fn(refs[:n_in], refs[n_in + final + 1:])
    if final:
        ms = jnp.mean(xn * xn, axis=-1, keepdims=True)
        xn = xn * lax.rsqrt(ms + NORM_EPS) * refs[n_in][...]
    refs[n_in + final][0] = xn


def _tile_spec(tm, d):
    return pl.BlockSpec((1, tm, d), lambda bi, i: (bi, i, 0))


def _outproj_common_specs(tm, d):
    return [
        _resident((d, d)),
        _tile_spec(tm, d),
        pl.BlockSpec((1, 1, d), lambda bi, i: (bi, 0, 0)),
    ]


def _outproj_call(res, tm, final_g=None):
    b, s, d = res.shape
    final = final_g is not None
    specs = list(res.specs) + ([_resident((1, d))] if final else [])
    args = tuple(res.args) + ((final_g.reshape(1, d),) if final else ())
    return pl.pallas_call(
        functools.partial(_outproj_body, res.fn, len(res.specs), int(final)),
        grid=(b, s // tm),
        in_specs=specs,
        out_specs=_tile_spec(tm, d),
        out_shape=jax.ShapeDtypeStruct((b, s, d), F32),
        scratch_shapes=list(res.scratch),
        compiler_params=_cparams(("parallel", "parallel")),
    )(*args)


def _fill_window(win_ref, prev_ref, cur_ref, next_ref, half, tq_blk):
    win_ref[0:half] = prev_ref[0]
    win_ref[half:half + tq_blk] = cur_ref[0]
    win_ref[half + tq_blk:2 * half + tq_blk] = next_ref[0]


def _band_mask(half, tq, j, n_sub, blk, n_blk):
    nk = tq + 2 * half
    row = lax.broadcasted_iota(jnp.int32, (tq, nk), 0)
    col = lax.broadcasted_iota(jnp.int32, (tq, nk), 1)
    delta = col - row
    mask = (delta >= 0) & (delta <= 2 * half)
    if j == 0:
        mask = mask & ((col >= half) | (blk > 0))
    if j == n_sub - 1:
        mask = mask & ((col < tq + half) | (blk < n_blk - 1))
    return mask


def _softmax_pv(s, mask, v, sink=None):
    s = jnp.where(mask, s, NEG_INF)
    m = jnp.max(s, axis=-1, keepdims=True)
    if sink is not None:
        m = jnp.maximum(m, sink)
    p = jnp.exp(s - m)
    l = jnp.sum(p, axis=-1, keepdims=True)
    if sink is not None:
        l = l + jnp.exp(sink - m)
    o = jnp.dot(p.astype(BF16), v, preferred_element_type=F32)
    return o / l, m, l


def _qk(q, k):
    return lax.dot_general(q, k, (((1,), (1,)), ((), ())), preferred_element_type=F32)


def _dil_attn_body(half, n_heads, q_ref, kp_ref, kc_ref, kn_ref, vp_ref, vc_ref, vn_ref,
                   o_ref, lse_ref, kwin_ref, vwin_ref):
    blk = pl.program_id(2)
    n_blk = pl.num_programs(2)
    tq_blk = q_ref.shape[1]
    tq = 2 * half
    n_sub = tq_blk // tq
    _fill_window(kwin_ref, kp_ref, kc_ref, kn_ref, half, tq_blk)
    _fill_window(vwin_ref, vp_ref, vc_ref, vn_ref, half, tq_blk)
    lane = lax.broadcasted_iota(jnp.int32, (tq, LANES), 1)
    for j in range(n_sub):
        mask = _band_mask(half, tq, j, n_sub, blk, n_blk)
        r0 = j * tq
        lse_tile = jnp.zeros((tq, LANES), F32)
        for h in range(n_heads):
            cs = slice(h * LANES, (h + 1) * LANES)
            s = _qk(q_ref[0, r0:r0 + tq, cs], kwin_ref[r0:r0 + tq + 2 * half, cs])
            o, m, l = _softmax_pv(s, mask, vwin_ref[r0:r0 + tq + 2 * half, cs])
            o_ref[0, r0:r0 + tq, cs] = o.astype(o_ref.dtype)
            lse_tile = jnp.where(lane == h, m + jnp.log(l), lse_tile)
        lse_ref[0, r0:r0 + tq, :] = lse_tile


def _dil_attention(q, k, v, half, tq_blk):
    b, rate, l, c = v.shape
    tq_blk = min(tq_blk, l)
    n_blk = l // tq_blk
    hb = tq_blk // half
    last_halo = l // half - 1
    cur = pl.BlockSpec((1, None, tq_blk, c), lambda bi, p, i: (bi, p, i, 0))
    prev = pl.BlockSpec((1, None, half, c), lambda bi, p, i: (bi, p, jnp.maximum(i * hb - 1, 0), 0))
    nxt = pl.BlockSpec((1, None, half, c),
                       lambda bi, p, i: (bi, p, jnp.minimum((i + 1) * hb, last_halo), 0))
    body = functools.partial(_dil_attn_body, half, c // LANES)
    return pl.pallas_call(
        body,
        grid=(b, rate, n_blk),
        in_specs=[cur, prev, cur, nxt, prev, cur, nxt],
        out_specs=[cur, pl.BlockSpec((1, None, tq_blk, LANES), lambda bi, p, i: (bi, p, i, 0))],
        out_shape=[jax.ShapeDtypeStruct((b, rate, l, c), BF16),
                   jax.ShapeDtypeStruct((b, rate, l, LANES), F32)],
        scratch_shapes=[pltpu.VMEM((tq_blk + 2 * half, c), BF16),
                        pltpu.VMEM((tq_blk + 2 * half, c), BF16)],
        compiler_params=_cparams(("parallel", "parallel", "parallel")),
    )(q, k, k, k, v, v, v)


def _swa_attn_body(half, n_pairs, q_per_kv, sink_ref, q_ref, kp_ref, kc_ref, kn_ref,
                   vp_ref, vc_ref, vn_ref, o_ref, kwin_ref, vwin_ref, s_ref):
    blk = pl.program_id(1)
    n_blk = pl.num_programs(1)
    tq_blk = q_ref.shape[1]
    tq = half
    nk = tq + 2 * half
    n_sub = tq_blk // tq
    n_stack = 2 * q_per_kv
    _fill_window(kwin_ref, kp_ref, kc_ref, kn_ref, half, tq_blk)
    for p in range(n_pairs):
        vs = slice(p * LANES, (p + 1) * LANES)
        ws = slice(2 * p * LANES, (2 * p + 1) * LANES)
        vwin_ref[0:half, ws] = vp_ref[0, :, vs]
        vwin_ref[half:half + tq_blk, ws] = vc_ref[0, :, vs]
        vwin_ref[half + tq_blk:2 * half + tq_blk, ws] = vn_ref[0, :, vs]
        vwin_ref[:, (2 * p + 1) * LANES:(2 * p + 2) * LANES] = jnp.ones(
            (tq_blk + 2 * half, LANES), BF16)
    lane = lax.broadcasted_iota(jnp.int32, (tq, LANES), 1)
    first = (lane % (LANES // 2)) < (LANES // 4)
    units = [(j, p) for j in range(n_sub) for p in range(n_pairs)]

    def scores(i):
        j, p = units[i]
        r0 = j * tq
        qs = []
        for g in range(q_per_kv):
            u = p * q_per_kv + g
            q = q_ref[0, r0:r0 + tq, u * LANES:(u + 1) * LANES]
            zero = jnp.zeros_like(q)
            qs += [jnp.where(first, q, zero), jnp.where(first, zero, q)]
        s_ref[i % 2] = _qk(jnp.concatenate(qs, axis=0),
                           kwin_ref[r0:r0 + nk, p * LANES:(p + 1) * LANES])

    scores(0)
    for i, (j, p) in enumerate(units):
        if i + 1 < len(units):
            scores(i + 1)
        r0 = j * tq
        mask = _band_mask(half, tq, j, n_sub, blk, n_blk)[None]
        sinks = []
        for g in range(q_per_kv):
            u = p * q_per_kv + g
            sinks += [jnp.full((tq, 1), sink_ref[2 * u], F32),
                      jnp.full((tq, 1), sink_ref[2 * u + 1], F32)]
        sink = jnp.concatenate(sinks, axis=0)
        s = jnp.where(mask, s_ref[i % 2].reshape(n_stack, tq, nk), NEG_INF).reshape(n_stack * tq, nk)
        m = jnp.maximum(jnp.max(s, axis=-1, keepdims=True), sink)
        pr = jnp.exp2((s - m).astype(BF16))
        out = jnp.dot(pr, vwin_ref[r0:r0 + nk, 2 * p * LANES:(2 * p + 2) * LANES],
                      preferred_element_type=F32)
        o = out[:, 0:LANES] / (out[:, LANES:2 * LANES] + jnp.exp2(sink - m))
        for g in range(q_per_kv):
            u = p * q_per_kv + g
            oa = o[2 * g * tq:(2 * g + 1) * tq]
            ob = o[(2 * g + 1) * tq:(2 * g + 2) * tq]
            o_ref[0, r0:r0 + tq, u * LANES:(u + 1) * LANES] = jnp.where(
                lane < LANES // 2, oa, ob).astype(o_ref.dtype)


def _swa_attention(q, k, v, sink_pairs, half, tq_blk, q_per_kv):
    b, s, cq = q.shape
    ck = k.shape[-1]
    tq_blk = min(tq_blk, s)
    n_blk = s // tq_blk
    hb = tq_blk // half
    last_halo = s // half - 1
    qspec = pl.BlockSpec((1, tq_blk, cq), lambda bi, i: (bi, i, 0))
    cur = pl.BlockSpec((1, tq_blk, ck), lambda bi, i: (bi, i, 0))
    prev = pl.BlockSpec((1, half, ck), lambda bi, i: (bi, jnp.maximum(i * hb - 1, 0), 0))
    nxt = pl.BlockSpec((1, half, ck), lambda bi, i: (bi, jnp.minimum((i + 1) * hb, last_halo), 0))
    body = functools.partial(_swa_attn_body, half, ck // LANES, q_per_kv)
    return pl.pallas_call(
        body,
        grid=(b, n_blk),
        in_specs=[pl.BlockSpec(memory_space=pltpu.SMEM), qspec, prev, cur, nxt, prev, cur, nxt],
        out_specs=qspec,
        out_shape=jax.ShapeDtypeStruct((b, s, cq), BF16),
        scratch_shapes=[pltpu.VMEM((tq_blk + 2 * half, ck), BF16),
                        pltpu.VMEM((tq_blk + 2 * half, 2 * ck), BF16),
                        pltpu.VMEM((2, 2 * q_per_kv * half, 3 * half), F32)],
        compiler_params=_cparams(("parallel", "parallel")),
    )(sink_pairs, q, k, k, k, v, v, v)


def _diff_attn_body(tk, lam_init, lam_ref, sg_ref, q_ref, k_ref, v_ref, o_ref,
                    vt_ref, qt_ref, st_ref, pt_ref, acc_ref):
    tq = q_ref.shape[1]
    n_kv = k_ref.shape[1] // tk

    @pl.when(pl.program_id(2) == 0)
    def _():
        for j in range(n_kv):
            vt_ref[j, 0:LANES] = v_ref[0, j * tk:(j + 1) * tk, :].astype(F32).T.astype(BF16)
            vt_ref[j, LANES:] = jnp.ones((vt_ref.shape[1] - LANES, tk), BF16)

    row = lax.broadcasted_iota(jnp.int32, (LANES, tq), 0)
    first = (row % (LANES // 2)) < (LANES // 4)
    qt = q_ref[0].astype(F32).T
    qt_ref[:, 0:tq] = jnp.where(first, qt, 0.0).astype(BF16)
    qt_ref[:, tq:2 * tq] = jnp.where(first, 0.0, qt).astype(BF16)
    acc_ref[...] = jnp.zeros(acc_ref.shape, F32)

    def fold(x, op):
        while x.shape[0] > 8:
            half = x.shape[0] // 2
            x = op(x[:half], x[half:])
        return x

    def scores(kk, slot):
        k0 = kk * tk if isinstance(kk, int) else pl.multiple_of(kk * tk, tk)
        st_ref[slot] = jnp.dot(k_ref[0, pl.ds(k0, tk), :], qt_ref[...],
                               preferred_element_type=F32)

    def softmax(slot, m_old):
        st = st_ref[slot]
        m_new = jnp.maximum(m_old, jnp.max(fold(st, jnp.maximum), axis=0, keepdims=True))
        pt_ref[slot] = jnp.exp2((st - m_new).astype(BF16))
        return m_new, jnp.exp2(m_old - m_new)

    depth = st_ref.shape[0]

    def tile_step(kk, slot, state, more_scores=True):
        m, alpha = state
        pv = jnp.dot(vt_ref[kk], pt_ref[slot], preferred_element_type=F32)
        if more_scores:
            scores(kk + depth, slot)
        state = softmax((slot + 1) % depth, m)
        acc_ref[...] = alpha * acc_ref[...] + pv
        return state

    def step(i, state):
        for j in range(depth):
            state = tile_step(depth * i + j, j, state)
        return state

    for j in range(depth):
        scores(j, j)
    state = softmax(0, jnp.full((1, 2 * tq), NEG_INF, F32))
    n_loop = (n_kv - depth) // depth
    state = lax.fori_loop(0, n_loop, step, state)
    for kk in range(n_loop * depth, n_kv - 1):
        state = tile_step(kk, kk % depth, state, more_scores=kk + depth < n_kv)
    _, alpha = state
    acc_ref[...] = alpha * acc_ref[...] + jnp.dot(vt_ref[n_kv - 1], pt_ref[(n_kv - 1) % depth],
                                                  preferred_element_type=F32)

    lv = lam_ref[...]
    lam = (jnp.exp(jnp.sum(lv[0:1] * lv[1:2], axis=-1, keepdims=True))
           - jnp.exp(jnp.sum(lv[2:3] * lv[3:4], axis=-1, keepdims=True)) + lam_init)
    o_all = acc_ref[0:LANES] / acc_ref[LANES:LANES + 1]
    ot = o_all[:, 0:tq] - lam * o_all[:, tq:2 * tq]
    ms = jnp.mean(ot * ot, axis=0, keepdims=True)
    ot = ot * lax.rsqrt(ms + SUBLN_EPS) * sg_ref[...] * (1.0 - lam_init)
    o_ref[0] = ot.T.astype(o_ref.dtype)


def _diff_attention(q, k, v, lam_vecs, subln_g, lam_init, tq, tk, depth=3):
    b, s, c = q.shape
    tq = min(tq, s)
    tk = min(tk, s)
    assert s // tk > depth
    n_heads = c // LANES
    body = functools.partial(_diff_attn_body, tk, lam_init)
    qspec = pl.BlockSpec((1, tq, LANES), lambda bi, h, i: (bi, i, h))
    kvspec = pl.BlockSpec((1, s, LANES), lambda bi, h, i: (bi, 0, h))
    sg = jnp.broadcast_to(subln_g[:, None], (LANES, tq))
    return pl.pallas_call(
        body,
        grid=(b, n_heads, s // tq),
        in_specs=[_resident(lam_vecs.shape), _resident((LANES, tq)), qspec, kvspec, kvspec],
        out_specs=qspec,
        out_shape=jax.ShapeDtypeStruct((b, s, c), BF16),
        scratch_shapes=[pltpu.VMEM((s // tk, LANES + BF16_ROWS, tk), BF16),
                        pltpu.VMEM((LANES, 2 * tq), BF16),
                        pltpu.VMEM((depth, tk, 2 * tq), F32),
                        pltpu.VMEM((depth, tk, 2 * tq), BF16),
                        pltpu.VMEM((LANES + BF16_ROWS, 2 * tq), F32)],
        compiler_params=_cparams(("parallel", "parallel", "arbitrary")),
    )(lam_vecs, sg, q, k, v)


def _rope_tables(seq, dim, scale):
    inv = ROPE_THETA ** (-jnp.arange(0, dim, 2, dtype=F32) / dim)
    ang = jnp.arange(seq, dtype=F32)[:, None] * inv[None, :]
    reps = (LANES // 2) // (dim // 2)
    cos = jnp.tile(jnp.cos(ang), (1, 2 * reps))
    sin = jnp.tile(jnp.sin(ang), (1, reps))
    sin = jnp.concatenate([-sin, sin], axis=1)
    return cos, sin, cos * scale, sin * scale


def _pair_interleave(a0, b0, half_dim):
    r = np.arange(half_dim)
    return np.concatenate([a0 + r, b0 + r, a0 + half_dim + r, b0 + half_dim + r])


def _cast_w(w, cols=None):
    if cols is not None:
        w = w[:, cols]
    return w.astype(BF16)


def _conv_layer(x, shift, scale, gate, g, w_in, conv_k, w_out, tm):
    b, s, d = x.shape
    e = w_out.shape[0]
    ops = [("mul", (e, 2 * e), e, [(0, 0, 1)], None, 1),
           ("mulsilu", (0, 3 * e), e, [(1, 0, 1)], None, 1)]
    x, (t, gz) = _inproj(x, shift, scale, g, _cast_w(w_in), [], [(e, 1), (e, 1)], ops, tm)
    halo = 16
    hb = tm // halo
    last = s // halo - 1
    specs = [
        pl.BlockSpec((1, halo, e), lambda bi, i: (bi, jnp.maximum(i * hb - 1, 0), 0)),
        _tile_spec(tm, e),
        pl.BlockSpec((1, halo, e), lambda bi, i: (bi, jnp.minimum((i + 1) * hb, last), 0)),
        _tile_spec(tm, e),
        _resident(conv_k.shape),
    ] + _outproj_common_specs(tm, d)
    return _Residual(functools.partial(_conv_residual, halo), specs,
                     (t, t, t, gz, conv_k, _cast_w(w_out), x, gate),
                     [pltpu.VMEM((tm + 2 * halo, e), F32)], (b, s, d))


def _dil_layer(x, shift, scale, gate, g, w_in, w_out, tm):
    if isinstance(x, _Residual):
        x = _outproj_call(x, tm)
    b, s, d = x.shape
    n_grp = len(DIL_RATES)
    e = w_out.shape[0]
    dh = e // DIL_HEADS
    cos, sin, cos_q, sin_q = _rope_tables(s, dh, dh ** -0.5)
    outs = [(e, r) for r in DIL_RATES] * 3 + [(e, 1)]
    ops, tables = [], []
    for gi, r in enumerate(DIL_RATES):
        grouped = lambda t: t if r == 1 else t.reshape(s // r, r, LANES).transpose(1, 0, 2)
        tables += [grouped(t) for t in (cos, sin, cos_q, sin_q)]
        ops.append(("rope", (gi * e,), e, [(gi, 0, r)], (4 * gi + 2, 4 * gi + 3), r))
        ops.append(("rope", ((n_grp + gi) * e,), e, [(n_grp + gi, 0, r)], (4 * gi, 4 * gi + 1), r))
    ops.append(("plain", (2 * n_grp * e,), e,
                [(2 * n_grp + gi, 0, r) for gi, r in enumerate(DIL_RATES)], None, 1))
    ops.append(("silu", (2 * n_grp * e + e,), e, [(3 * n_grp, 0, 1)], None, 1))
    x, res = _inproj(x, shift, scale, g, _cast_w(w_in), tables, outs, ops, tm)
    zs = res[3 * n_grp]
    os_, lses = [], []
    for gi, rate in enumerate(DIL_RATES):
        half = DIL_WINDOWS[gi] // (2 * rate)
        grouped = lambda t: t.reshape(b, rate, s // rate, e)
        o, lse = _dil_attention(grouped(res[gi]), grouped(res[n_grp + gi]),
                                grouped(res[2 * n_grp + gi]), half, 512)
        os_.append(o)
        lses.append(lse)
    rows = np.arange(2 * LANES)[:, None] % LANES
    expand = jnp.asarray(rows == (np.arange(e)[None, :] // dh), dtype=BF16)
    grouped_spec = lambda r, wd: pl.BlockSpec((1, r, tm // r, wd), lambda bi, i: (bi, 0, i, 0))
    specs = ([grouped_spec(r, e) for r in DIL_RATES] + [grouped_spec(r, LANES) for r in DIL_RATES]
             + [_tile_spec(tm, e), _resident(expand.shape)] + _outproj_common_specs(tm, d))
    scratch = ([pltpu.VMEM((e // LANES, tm, LANES), F32)] * n_grp
               + [pltpu.VMEM((1, tm, LANES), F32)] * n_grp)
    return _Residual(functools.partial(_dil_residual, n_grp), specs,
                     (*os_, *lses, zs, expand, _cast_w(w_out), x, gate), scratch, (b, s, d))


def _swa_layer(x, shift, scale, gate, g, w_in, sink, w_out, tm):
    b, s, d = x.shape
    hq, hk, dh = SWA_Q_HEADS, SWA_KV_HEADS, SWA_HEAD_DIM
    grp = hq // hk
    pairs = [((2 * p) * grp + gi, (2 * p + 1) * grp + gi) for p in range(hk // 2) for gi in range(grp)]
    q_cols = np.concatenate([_pair_interleave(a * dh, b_ * dh, dh // 2) for a, b_ in pairs])
    k_cols = hq * dh + np.concatenate(
        [_pair_interleave(2 * p * dh, (2 * p + 1) * dh, dh // 2) for p in range(hk // 2)])
    v_cols = hq * dh + hk * dh + np.arange(hk * dh)
    head_cols = np.concatenate([np.concatenate([a * dh + np.arange(dh), b_ * dh + np.arange(dh)])
                                for a, b_ in pairs])
    z_cols = hq * dh + 2 * hk * dh + head_cols
    w = _cast_w(w_in, np.concatenate([q_cols, k_cols, v_cols, z_cols]))
    sink_pairs = sink[np.array([h for pr in pairs for h in pr])] * math.log2(math.e)
    cos, sin, cos_q, sin_q = _rope_tables(s, dh, dh ** -0.5 * math.log2(math.e))
    nq, nk = hq * dh, hk * dh
    ops = [("rope", (0,), nq, [(0, 0, 1)], (2, 3), 1),
           ("rope", (nq,), nk, [(1, 0, 1)], (0, 1), 1),
           ("plain", (nq + nk,), nk, [(2, 0, 1)], None, 1),
           ("silu", (nq + 2 * nk,), nq, [(3, 0, 1)], None, 1)]
    x, (q, k, v, zs) = _inproj(x, shift, scale, g, w, [cos, sin, cos_q, sin_q],
                               [(nq, 1), (nk, 1), (nk, 1), (nq, 1)], ops, tm)
    o = _swa_attention(q, k, v, sink_pairs, SWA_HALF, 512, grp)
    specs = [_tile_spec(tm, nq), _tile_spec(tm, nq)] + _outproj_common_specs(tm, d)
    return _Residual(_mul_residual, specs, (o, zs, _cast_w(w_out[head_cols]), x, gate), [], (b, s, d))


def _diff_layer(x, shift, scale, gate, g, w_in, lam_vecs, subln_g, w_out, layer_idx, tm):
    b, s, d = x.shape
    nh, dh = DIFF_HEADS, DIFF_HEAD_DIM
    e = nh * 2 * dh
    qk_cols = np.concatenate([_pair_interleave(h * 2 * dh, h * 2 * dh + dh, dh // 2) for h in range(nh)])
    cols = np.concatenate([qk_cols, e + qk_cols, 2 * e + np.arange(2 * e)])
    w = _cast_w(w_in, cols)
    cos, sin, cos_q, sin_q = _rope_tables(s, dh, dh ** -0.5 * math.log2(math.e))
    ops = [("rope", (0,), e, [(0, 0, 1)], (2, 3), 1),
           ("rope", (e,), e, [(1, 0, 1)], (0, 1), 1),
           ("plain", (2 * e,), e, [(2, 0, 1)], None, 1),
           ("silu", (3 * e,), e, [(3, 0, 1)], None, 1)]
    x, (q, k, v, zs) = _inproj(x, shift, scale, g, w, [cos, sin, cos_q, sin_q], [(e, 1)] * 4, ops, tm)
    lam_init = 0.8 - 0.6 * math.exp(-0.3 * layer_idx)
    o = _diff_attention(q, k, v, lam_vecs, subln_g, lam_init, 512, 1024)
    specs = [_tile_spec(tm, e), _tile_spec(tm, e)] + _outproj_common_specs(tm, d)
    return _Residual(_mul_residual, specs, (o, zs, _cast_w(w_out), x, gate), [], (b, s, d))


def kernel(x, c, norm_g, w_mod, b_mod, conv_w_in, conv_k, conv_w_out, dil_w_in, dil_w_out,
           swa_w_in, swa_sink, swa_w_out, diff_w_in, diff_lambda, diff_subln_g, diff_w_out,
           final_g):
    b, s, d = x.shape
    depth = norm_g.shape[0]
    tm = min(512, s)
    mod = _modulation(c, w_mod, b_mod)
    for i in range(depth):
        kind, j = i % N_MIXERS, i // N_MIXERS
        shift, scale, gate = (mod[i, t][:, None, :] for t in range(3))
        g = norm_g[i].reshape(1, d)
        if kind == 0:
            x = _conv_layer(x, shift, scale, gate, g, conv_w_in[j], conv_k[j], conv_w_out[j], tm)
        elif kind == 1:
            x = _dil_layer(x, shift, scale, gate, g, dil_w_in[j], dil_w_out[j], tm)
        elif kind == 2:
            x = _swa_layer(x, shift, scale, gate, g, swa_w_in[j], swa_sink[j], swa_w_out[j], tm)
        else:
            x = _diff_layer(x, shift, scale, gate, g, diff_w_in[j], diff_lambda[j], diff_subln_g[j],
                            diff_w_out[j], i, tm)
    return _outproj_call(x, tm, final_g)
```

```python
import functools
import math
from typing import Callable, NamedTuple

import numpy as np
import jax
import jax.numpy as jnp
from jax import lax
from jax.experimental import pallas as pl
from jax.experimental.pallas import tpu as pltpu

F32 = jnp.float32
BF16 = jnp.bfloat16

LANES = 128
BF16_ROWS = 16
VMEM_LIMIT = 56 * 1024 * 1024

N_MIXERS = 4
CONV_WIDTH = 3
DIL_WINDOWS = (128, 512, 2048)
DIL_RATES = (1, 4, 16)
DIL_HEADS = 8
SWA_HALF = 128
SWA_Q_HEADS = 16
SWA_KV_HEADS = 4
SWA_HEAD_DIM = 64
DIFF_HEADS = 8
DIFF_HEAD_DIM = 64
ROPE_THETA = 10000.0
NORM_EPS = 1e-6
SUBLN_EPS = 1e-5
NEG_INF = -1e30


def _cparams(sem):
    return pltpu.CompilerParams(dimension_semantics=sem, vmem_limit_bytes=VMEM_LIMIT)


def _resident(shape):
    nd = len(shape)
    return pl.BlockSpec(shape, lambda *_: (0,) * nd)


def _mod_body(c_ref, w_ref, b_ref, o_ref):
    c = c_ref[...]
    sc = c * jax.nn.sigmoid(c)
    o_ref[0, 0] = jnp.dot(sc, w_ref[0], precision=lax.Precision.HIGHEST,
                          preferred_element_type=F32) + b_ref[0, 0]


def _modulation(c, w_mod, b_mod):
    depth, d, _ = w_mod.shape
    b = c.shape[0]
    rows = -(-b // 8) * 8
    cp = jnp.pad(c, ((0, rows - b), (0, 0)))
    out = pl.pallas_call(
        _mod_body,
        grid=(depth, 3),
        in_specs=[
            pl.BlockSpec((rows, d), lambda i, j: (0, 0)),
            pl.BlockSpec((1, d, d), lambda i, j: (i, 0, j)),
            pl.BlockSpec((1, 1, 1, d), lambda i, j: (i, j, 0, 0)),
        ],
        out_specs=pl.BlockSpec((1, 1, rows, d), lambda i, j: (i, j, 0, 0)),
        out_shape=jax.ShapeDtypeStruct((depth, 3, rows, d), F32),
        compiler_params=_cparams(("arbitrary", "arbitrary")),
    )(cp, w_mod, b_mod.reshape(depth, 3, 1, d))
    return out[:, :, :b]


def _silu(a):
    return a * jax.nn.sigmoid(a)


def _inproj_body(ops, row_rates, n_tab, n_out, chunk, prev_fn, n_x, *refs):
    shift_ref, scale_ref, g_ref, w_ref = refs[n_x:n_x + 4]
    rest = refs[n_x + 4:]
    tabs = rest[:n_tab]
    outs = rest[n_tab:n_tab + n_out]
    n_own = 2 + (1 + len(row_rates) if row_rates else 0)
    h_ref, tmp_ref = rest[n_tab + n_out:n_tab + n_out + 2]
    h_by_rate = {1: h_ref}
    if prev_fn is None:
        x = refs[0][0]
    else:
        x = prev_fn(refs[:n_x], rest[n_tab + n_out + n_own:])
        outs[0][0] = x
        outs = outs[1:]
    tm, d = x.shape
    ms = jnp.mean(x * x, axis=-1, keepdims=True)
    y = x * lax.rsqrt(ms + NORM_EPS) * g_ref[...]
    h = y * (1.0 + scale_ref[0]) + shift_ref[0]
    h_ref[...] = h.astype(BF16)
    if row_rates:
        stage_ref = rest[n_tab + n_out + 2]
        for j in range(d // LANES):
            stage_ref[j] = h[:, j * LANES:(j + 1) * LANES]
        for rate, hp_ref in zip(row_rates, rest[n_tab + n_out + 3:]):
            h_by_rate[rate] = hp_ref
            n_rows = tm // rate
            for p in range(rate):
                for j in range(d // LANES):
                    hp_ref[p * n_rows:(p + 1) * n_rows, j * LANES:(j + 1) * LANES] = (
                        stage_ref[j, pl.ds(p, n_rows, stride=rate), :].astype(BF16))

    for kind, srcs, ncols, dests, tab, row_rate in ops:
        src_ref = h_by_rate[row_rate]

        def proj(c0, n, src_ref=src_ref):
            return jnp.dot(src_ref[...], w_ref[:, c0:c0 + n], preferred_element_type=F32)

        for c in range(0, ncols, chunk):
            n = min(chunk, ncols - c)
            if kind == "plain":
                r = proj(srcs[0] + c, n)
            elif kind == "silu":
                r = _silu(proj(srcs[0] + c, n))
            elif kind == "mul":
                r = proj(srcs[0] + c, n) * proj(srcs[1] + c, n)
            elif kind == "mulsilu":
                r = proj(srcs[0] + c, n) * _silu(proj(srcs[1] + c, n))
            elif kind == "rope":
                a = proj(srcs[0] + c, n)
                cos = tabs[tab[0]][...].reshape(tm, LANES)
                sin = tabs[tab[1]][...].reshape(tm, LANES)
                parts = []
                for j in range(n // LANES):
                    aj = a[:, j * LANES:(j + 1) * LANES]
                    parts.append(aj * cos + pltpu.roll(aj, LANES // 2, 1) * sin)
                r = parts[0] if len(parts) == 1 else jnp.concatenate(parts, axis=1)
            else:
                raise ValueError(kind)
            staged = False
            for oi, ooff, rate in dests:
                o_ref = outs[oi]
                cols = slice(ooff + c, ooff + c + n)
                if rate == 1:
                    o_ref[0, :, cols] = r.astype(o_ref.dtype)
                    continue
                if rate == row_rate:
                    n_rows = tm // rate
                    for p in range(rate):
                        o_ref[0, p, :, cols] = r[p * n_rows:(p + 1) * n_rows].astype(o_ref.dtype)
                    continue
                assert row_rate == 1
                if not staged:
                    for j in range(n // LANES):
                        tmp_ref[j] = r[:, j * LANES:(j + 1) * LANES]
                    staged = True
                for p in range(rate):
                    for j in range(n // LANES):
                        o_ref[0, p, :, ooff + c + j * LANES:ooff + c + (j + 1) * LANES] = (
                            tmp_ref[j, pl.ds(p, tm // rate, stride=rate), :].astype(o_ref.dtype))


def _inproj(x, shift, scale, g, w, tables, outs, ops, tm, chunk=512):
    fused = isinstance(x, _Residual)
    b, s, d = x.shape
    n = w.shape[1]
    x_specs, x_args = (list(x.specs), tuple(x.args)) if fused else ([_tile_spec(tm, d)], (x,))
    in_specs = x_specs + [
        pl.BlockSpec((1, 1, d), lambda bi, i: (bi, 0, 0)),
        pl.BlockSpec((1, 1, d), lambda bi, i: (bi, 0, 0)),
        _resident((1, d)),
        pl.BlockSpec((d, n), lambda bi, i: (0, 0), pipeline_mode=pl.Buffered(1)),
    ]
    for t in tables:
        if t.ndim == 2:
            in_specs.append(pl.BlockSpec((tm, LANES), lambda bi, i: (i, 0)))
        else:
            in_specs.append(pl.BlockSpec((t.shape[0], tm // t.shape[0], LANES),
                                         lambda bi, i: (0, i, 0)))
    row_rates = tuple(sorted({op[5] for op in ops} - {1}))
    scratch = [pltpu.VMEM((tm, d), BF16), pltpu.VMEM((chunk // LANES, tm, LANES), F32)]
    if row_rates:
        scratch += [pltpu.VMEM((d // LANES, tm, LANES), F32)] + [pltpu.VMEM((tm, d), BF16)] * len(row_rates)
    out_specs, out_shape = [], []
    if fused:
        scratch += list(x.scratch)
        out_specs.append(_tile_spec(tm, d))
        out_shape.append(jax.ShapeDtypeStruct((b, s, d), F32))
    for wd, rate in outs:
        if rate == 1:
            out_specs.append(pl.BlockSpec((1, tm, wd), lambda bi, i: (bi, i, 0)))
            out_shape.append(jax.ShapeDtypeStruct((b, s, wd), BF16))
        else:
            out_specs.append(pl.BlockSpec((1, rate, tm // rate, wd), lambda bi, i: (bi, 0, i, 0)))
            out_shape.append(jax.ShapeDtypeStruct((b, rate, s // rate, wd), BF16))
    body = functools.partial(_inproj_body, tuple(ops), row_rates, len(tables), len(out_specs), chunk,
                             x.fn if fused else None, len(x_specs))
    res = pl.pallas_call(
        body,
        grid=(b, s // tm),
        in_specs=in_specs,
        out_specs=out_specs,
        out_shape=out_shape,
        scratch_shapes=scratch,
        compiler_params=_cparams(("parallel", "parallel")),
    )(*x_args, shift, scale, g, w, *tables)
    return (res[0], res[1:]) if fused else (x, res)


def _outproj_tail(y, w_ref, x_ref, gate_ref):
    acc = jnp.dot(y.astype(BF16), w_ref[...], preferred_element_type=F32)
    return x_ref[0] + gate_ref[0] * acc


def _conv_residual(halo, in_refs, scratch_refs):
    t_prev_ref, t_ref, t_next_ref, g_ref, ck_ref, w_ref, x_ref, gate_ref = in_refs
    win_ref, = scratch_refs
    i = pl.program_id(1)
    tm = t_ref.shape[1]
    has_prev = (i > 0).astype(F32)
    has_next = (i < pl.num_programs(1) - 1).astype(F32)
    win_ref[0:halo] = t_prev_ref[0].astype(F32) * has_prev
    win_ref[halo:halo + tm] = t_ref[0].astype(F32)
    win_ref[halo + tm:2 * halo + tm] = t_next_ref[0].astype(F32) * has_next
    ck = ck_ref[...]
    conv = (win_ref[halo - 1:halo - 1 + tm] * ck[0:1]
            + win_ref[halo:halo + tm] * ck[1:2]
            + win_ref[halo + 1:halo + 1 + tm] * ck[2:3])
    y = g_ref[0].astype(F32) * conv
    return _outproj_tail(y, w_ref, x_ref, gate_ref)


def _interleave(src_ref, stage_ref):
    rate = src_ref.shape[1]
    if rate == 1:
        return src_ref[0, 0].astype(F32)
    n = src_ref.shape[2]
    n_col = src_ref.shape[3] // LANES
    for p in range(rate):
        for j in range(n_col):
            stage_ref[j, pl.ds(p, n, stride=rate), :] = (
                src_ref[0, p, :, j * LANES:(j + 1) * LANES].astype(F32))
    if n_col == 1:
        return stage_ref[0]
    return jnp.concatenate([stage_ref[j] for j in range(n_col)], axis=1)


def _dil_residual(n_grp, in_refs, scratch_refs):
    o_refs = in_refs[:n_grp]
    lse_refs = in_refs[n_grp:2 * n_grp]
    zs_ref, expand_ref, w_ref, x_ref, gate_ref = in_refs[2 * n_grp:]
    o_stage = scratch_refs[:n_grp]
    lse_stage = scratch_refs[n_grp:]
    lses = [_interleave(r, st) for r, st in zip(lse_refs, lse_stage)]
    mx = functools.reduce(jnp.maximum, lses)
    es = [jnp.exp(l - mx) for l in lses]
    inv = 1.0 / functools.reduce(jnp.add, es)
    o = None
    for gi in range(n_grp):
        wt = es[gi] * inv
        hi = wt.astype(BF16)
        lo = (wt - hi.astype(F32)).astype(BF16)
        wfull = jnp.dot(jnp.concatenate([hi, lo], axis=1), expand_ref[...],
                        preferred_element_type=F32)
        term = wfull * _interleave(o_refs[gi], o_stage[gi])
        o = term if o is None else o + term
    y = o * zs_ref[0].astype(F32)
    return _outproj_tail(y, w_ref, x_ref, gate_ref)


def _mul_residual(in_refs, scratch_refs):
    o_ref, zs_ref, w_ref, x_ref, gate_ref = in_refs
    del scratch_refs
    y = o_ref[0].astype(F32) * zs_ref[0].astype(F32)
    return _outproj_tail(y, w_ref, x_ref, gate_ref)


class _Residual(NamedTuple):
    fn: Callable
    specs: list
    args: tuple
    scratch: list
    shape: tuple


def _outproj_body(fn, n_in, final, *refs):
    xn = fn(refs[:n_in], refs[n_in + final + 1:])
    if final:
        ms = jnp.mean(xn * xn, axis=-1, keepdims=True)
        xn = xn * lax.rsqrt(ms + NORM_EPS) * refs[n_in][...]
    refs[n_in + final][0] = xn


def _tile_spec(tm, d):
    return pl.BlockSpec((1, tm, d), lambda bi, i: (bi, i, 0))


def _outproj_common_specs(tm, d):
    return [
        _resident((d, d)),
        _tile_spec(tm, d),
        pl.BlockSpec((1, 1, d), lambda bi, i: (bi, 0, 0)),
    ]


def _outproj_call(res, tm, final_g=None):
    b, s, d = res.shape
    final = final_g is not None
    specs = list(res.specs) + ([_resident((1, d))] if final else [])
    args = tuple(res.args) + ((final_g.reshape(1, d),) if final else ())
    return pl.pallas_call(
        functools.partial(_outproj_body, res.fn, len(res.specs), int(final)),
        grid=(b, s // tm),
        in_specs=specs,
        out_specs=_tile_spec(tm, d),
        out_shape=jax.ShapeDtypeStruct((b, s, d), F32),
        scratch_shapes=list(res.scratch),
        compiler_params=_cparams(("parallel", "parallel")),
    )(*args)


def _fill_window(win_ref, prev_ref, cur_ref, next_ref, half, tq_blk):
    win_ref[0:half] = prev_ref[0]
    win_ref[half:half + tq_blk] = cur_ref[0]
    win_ref[half + tq_blk:2 * half + tq_blk] = next_ref[0]


def _band_mask(half, tq, j, n_sub, blk, n_blk):
    nk = tq + 2 * half
    row = lax.broadcasted_iota(jnp.int32, (tq, nk), 0)
    col = lax.broadcasted_iota(jnp.int32, (tq, nk), 1)
    delta = col - row
    mask = (delta >= 0) & (delta <= 2 * half)
    if j == 0:
        mask = mask & ((col >= half) | (blk > 0))
    if j == n_sub - 1:
        mask = mask & ((col < tq + half) | (blk < n_blk - 1))
    return mask


def _softmax_pv(s, mask, v, sink=None):
    s = jnp.where(mask, s, NEG_INF)
    m = jnp.max(s, axis=-1, keepdims=True)
    if sink is not None:
        m = jnp.maximum(m, sink)
    p = jnp.exp(s - m)
    l = jnp.sum(p, axis=-1, keepdims=True)
    if sink is not None:
        l = l + jnp.exp(sink - m)
    o = jnp.dot(p.astype(BF16), v, preferred_element_type=F32)
    return o / l, m, l


def _qk(q, k):
    return lax.dot_general(q, k, (((1,), (1,)), ((), ())), preferred_element_type=F32)


def _dil_attn_body(half, n_heads, q_ref, kp_ref, kc_ref, kn_ref, vp_ref, vc_ref, vn_ref,
                   o_ref, lse_ref, kwin_ref, vwin_ref):
    blk = pl.program_id(2)
    n_blk = pl.num_programs(2)
    tq_blk = q_ref.shape[1]
    tq = 2 * half
    n_sub = tq_blk // tq
    _fill_window(kwin_ref, kp_ref, kc_ref, kn_ref, half, tq_blk)
    _fill_window(vwin_ref, vp_ref, vc_ref, vn_ref, half, tq_blk)
    lane = lax.broadcasted_iota(jnp.int32, (tq, LANES), 1)
    for j in range(n_sub):
        mask = _band_mask(half, tq, j, n_sub, blk, n_blk)
        r0 = j * tq
        lse_tile = jnp.zeros((tq, LANES), F32)
        for h in range(n_heads):
            cs = slice(h * LANES, (h + 1) * LANES)
            s = _qk(q_ref[0, r0:r0 + tq, cs], kwin_ref[r0:r0 + tq + 2 * half, cs])
            o, m, l = _softmax_pv(s, mask, vwin_ref[r0:r0 + tq + 2 * half, cs])
            o_ref[0, r0:r0 + tq, cs] = o.astype(o_ref.dtype)
            lse_tile = jnp.where(lane == h, m + jnp.log(l), lse_tile)
        lse_ref[0, r0:r0 + tq, :] = lse_tile


def _dil_attention(q, k, v, half, tq_blk):
    b, rate, l, c = v.shape
    tq_blk = min(tq_blk, l)
    n_blk = l // tq_blk
    hb = tq_blk // half
    last_halo = l // half - 1
    cur = pl.BlockSpec((1, None, tq_blk, c), lambda bi, p, i: (bi, p, i, 0))
    prev = pl.BlockSpec((1, None, half, c), lambda bi, p, i: (bi, p, jnp.maximum(i * hb - 1, 0), 0))
    nxt = pl.BlockSpec((1, None, half, c),
                       lambda bi, p, i: (bi, p, jnp.minimum((i + 1) * hb, last_halo), 0))
    body = functools.partial(_dil_attn_body, half, c // LANES)
    return pl.pallas_call(
        body,
        grid=(b, rate, n_blk),
        in_specs=[cur, prev, cur, nxt, prev, cur, nxt],
        out_specs=[cur, pl.BlockSpec((1, None, tq_blk, LANES), lambda bi, p, i: (bi, p, i, 0))],
        out_shape=[jax.ShapeDtypeStruct((b, rate, l, c), BF16),
                   jax.ShapeDtypeStruct((b, rate, l, LANES), F32)],
        scratch_shapes=[pltpu.VMEM((tq_blk + 2 * half, c), BF16),
                        pltpu.VMEM((tq_blk + 2 * half, c), BF16)],
        compiler_params=_cparams(("parallel", "parallel", "parallel")),
    )(q, k, k, k, v, v, v)


def _swa_attn_body(half, n_pairs, q_per_kv, sink_ref, q_ref, kp_ref, kc_ref, kn_ref,
                   vp_ref, vc_ref, vn_ref, o_ref, kwin_ref, vwin_ref, s_ref):
    blk = pl.program_id(1)
    n_blk = pl.num_programs(1)
    tq_blk = q_ref.shape[1]
    tq = half
    nk = tq + 2 * half
    n_sub = tq_blk // tq
    n_stack = 2 * q_per_kv
    _fill_window(kwin_ref, kp_ref, kc_ref, kn_ref, half, tq_blk)
    for p in range(n_pairs):
        vs = slice(p * LANES, (p + 1) * LANES)
        ws = slice(2 * p * LANES, (2 * p + 1) * LANES)
        vwin_ref[0:half, ws] = vp_ref[0, :, vs]
        vwin_ref[half:half + tq_blk, ws] = vc_ref[0, :, vs]
        vwin_ref[half + tq_blk:2 * half + tq_blk, ws] = vn_ref[0, :, vs]
        vwin_ref[:, (2 * p + 1) * LANES:(2 * p + 2) * LANES] = jnp.ones(
            (tq_blk + 2 * half, LANES), BF16)
    lane = lax.broadcasted_iota(jnp.int32, (tq, LANES), 1)
    first = (lane % (LANES // 2)) < (LANES // 4)
    units = [(j, p) for j in range(n_sub) for p in range(n_pairs)]

    def scores(i):
        j, p = units[i]
        r0 = j * tq
        qs = []
        for g in range(q_per_kv):
            u = p * q_per_kv + g
            q = q_ref[0, r0:r0 + tq, u * LANES:(u + 1) * LANES]
            zero = jnp.zeros_like(q)
            qs += [jnp.where(first, q, zero), jnp.where(first, zero, q)]
        s_ref[i % 2] = _qk(jnp.concatenate(qs, axis=0),
                           kwin_ref[r0:r0 + nk, p * LANES:(p + 1) * LANES])

    scores(0)
    for i, (j, p) in enumerate(units):
        if i + 1 < len(units):
            scores(i + 1)
        r0 = j * tq
        mask = _band_mask(half, tq, j, n_sub, blk, n_blk)[None]
        sinks = []
        for g in range(q_per_kv):
            u = p * q_per_kv + g
            sinks += [jnp.full((tq, 1), sink_ref[2 * u], F32),
                      jnp.full((tq, 1), sink_ref[2 * u + 1], F32)]
        sink = jnp.concatenate(sinks, axis=0)
        s = jnp.where(mask, s_ref[i % 2].reshape(n_stack, tq, nk), NEG_INF).reshape(n_stack * tq, nk)
        m = jnp.maximum(jnp.max(s, axis=-1, keepdims=True), sink)
        pr = jnp.exp2((s - m).astype(BF16))
        out = jnp.dot(pr, vwin_ref[r0:r0 + nk, 2 * p * LANES:(2 * p + 2) * LANES],
                      preferred_element_type=F32)
        o = out[:, 0:LANES] / (out[:, LANES:2 * LANES] + jnp.exp2(sink - m))
        for g in range(q_per_kv):
            u = p * q_per_kv + g
            oa = o[2 * g * tq:(2 * g + 1) * tq]
            ob = o[(2 * g + 1) * tq:(2 * g + 2) * tq]
            o_ref[0, r0:r0 + tq, u * LANES:(u + 1) * LANES] = jnp.where(
                lane < LANES // 2, oa, ob).astype(o_ref.dtype)


def _swa_attention(q, k, v, sink_pairs, half, tq_blk, q_per_kv):
    b, s, cq = q.shape
    ck = k.shape[-1]
    tq_blk = min(tq_blk, s)
    n_blk = s // tq_blk
    hb = tq_blk // half
    last_halo = s // half - 1
    qspec = pl.BlockSpec((1, tq_blk, cq), lambda bi, i: (bi, i, 0))
    cur = pl.BlockSpec((1, tq_blk, ck), lambda bi, i: (bi, i, 0))
    prev = pl.BlockSpec((1, half, ck), lambda bi, i: (bi, jnp.maximum(i * hb - 1, 0), 0))
    nxt = pl.BlockSpec((1, half, ck), lambda bi, i: (bi, jnp.minimum((i + 1) * hb, last_halo), 0))
    body = functools.partial(_swa_attn_body, half, ck // LANES, q_per_kv)
    return pl.pallas_call(
        body,
        grid=(b, n_blk),
        in_specs=[pl.BlockSpec(memory_space=pltpu.SMEM), qspec, prev, cur, nxt, prev, cur, nxt],
        out_specs=qspec,
        out_shape=jax.ShapeDtypeStruct((b, s, cq), BF16),
        scratch_shapes=[pltpu.VMEM((tq_blk + 2 * half, ck), BF16),
                        pltpu.VMEM((tq_blk + 2 * half, 2 * ck), BF16),
                        pltpu.VMEM((2, 2 * q_per_kv * half, 3 * half), F32)],
        compiler_params=_cparams(("parallel", "parallel")),
    )(sink_pairs, q, k, k, k, v, v, v)


def _diff_attn_body(tk, lam_init, lam_ref, sg_ref, q_ref, k_ref, v_ref, o_ref,
                    vt_ref, qt_ref, st_ref, pt_ref, acc_ref):
    tq = q_ref.shape[1]
    n_kv = k_ref.shape[1] // tk

    @pl.when(pl.program_id(2) == 0)
    def _():
        for j in range(n_kv):
            vt_ref[j, 0:LANES] = v_ref[0, j * tk:(j + 1) * tk, :].astype(F32).T.astype(BF16)
            vt_ref[j, LANES:] = jnp.ones((vt_ref.shape[1] - LANES, tk), BF16)

    row = lax.broadcasted_iota(jnp.int32, (LANES, tq), 0)
    first = (row % (LANES // 2)) < (LANES // 4)
    qt = q_ref[0].astype(F32).T
    qt_ref[:, 0:tq] = jnp.where(first, qt, 0.0).astype(BF16)
    qt_ref[:, tq:2 * tq] = jnp.where(first, 0.0, qt).astype(BF16)
    acc_ref[...] = jnp.zeros(acc_ref.shape, F32)

    def fold(x, op):
        while x.shape[0] > 8:
            half = x.shape[0] // 2
            x = op(x[:half], x[half:])
        return x

    def scores(kk, slot):
        k0 = kk * tk if isinstance(kk, int) else pl.multiple_of(kk * tk, tk)
        st_ref[slot] = jnp.dot(k_ref[0, pl.ds(k0, tk), :], qt_ref[...],
                               preferred_element_type=F32)

    def softmax(slot, m_old):
        st = st_ref[slot]
        m_new = jnp.maximum(m_old, jnp.max(fold(st, jnp.maximum), axis=0, keepdims=True))
        pt_ref[slot] = jnp.exp2(st - m_new).astype(BF16)
        return m_new, jnp.exp2(m_old - m_new)

    depth = st_ref.shape[0]

    def tile_step(kk, slot, state, more_scores=True):
        m, alpha = state
        pv = jnp.dot(vt_ref[kk], pt_ref[slot], preferred_element_type=F32)
        if more_scores:
            scores(kk + depth, slot)
        state = softmax((slot + 1) % depth, m)
        acc_ref[...] = alpha * acc_ref[...] + pv
        return state

    def step(i, state):
        for j in range(depth):
            state = tile_step(depth * i + j, j, state)
        return state

    for j in range(depth):
        scores(j, j)
    state = softmax(0, jnp.full((1, 2 * tq), NEG_INF, F32))
    n_loop = (n_kv - depth) // depth
    state = lax.fori_loop(0, n_loop, step, state)
    for kk in range(n_loop * depth, n_kv - 1):
        state = tile_step(kk, kk % depth, state, more_scores=kk + depth < n_kv)
    _, alpha = state
    acc_ref[...] = alpha * acc_ref[...] + jnp.dot(vt_ref[n_kv - 1], pt_ref[(n_kv - 1) % depth],
                                                  preferred_element_type=F32)

    lv = lam_ref[...]
    lam = (jnp.exp(jnp.sum(lv[0:1] * lv[1:2], axis=-1, keepdims=True))
           - jnp.exp(jnp.sum(lv[2:3] * lv[3:4], axis=-1, keepdims=True)) + lam_init)
    o_all = acc_ref[0:LANES] / acc_ref[LANES:LANES + 1]
    ot = o_all[:, 0:tq] - lam * o_all[:, tq:2 * tq]
    ms = jnp.mean(ot * ot, axis=0, keepdims=True)
    ot = ot * lax.rsqrt(ms + SUBLN_EPS) * sg_ref[...] * (1.0 - lam_init)
    o_ref[0] = ot.T.astype(o_ref.dtype)


def _diff_attention(q, k, v, lam_vecs, subln_g, lam_init, tq, tk, depth=3):
    b, s, c = q.shape
    tq = min(tq, s)
    tk = min(tk, s)
    assert s // tk > depth
    n_heads = c // LANES
    body = functools.partial(_diff_attn_body, tk, lam_init)
    qspec = pl.BlockSpec((1, tq, LANES), lambda bi, h, i: (bi, i, h))
    kvspec = pl.BlockSpec((1, s, LANES), lambda bi, h, i: (bi, 0, h))
    sg = jnp.broadcast_to(subln_g[:, None], (LANES, tq))
    return pl.pallas_call(
        body,
        grid=(b, n_heads, s // tq),
        in_specs=[_resident(lam_vecs.shape), _resident((LANES, tq)), qspec, kvspec, kvspec],
        out_specs=qspec,
        out_shape=jax.ShapeDtypeStruct((b, s, c), BF16),
        scratch_shapes=[pltpu.VMEM((s // tk, LANES + BF16_ROWS, tk), BF16),
                        pltpu.VMEM((LANES, 2 * tq), BF16),
                        pltpu.VMEM((depth, tk, 2 * tq), F32),
                        pltpu.VMEM((depth, tk, 2 * tq), BF16),
                        pltpu.VMEM((LANES + BF16_ROWS, 2 * tq), F32)],
        compiler_params=_cparams(("parallel", "parallel", "arbitrary")),
    )(lam_vecs, sg, q, k, v)


def _rope_tables(seq, dim, scale):
    inv = ROPE_THETA ** (-jnp.arange(0, dim, 2, dtype=F32) / dim)
    ang = jnp.arange(seq, dtype=F32)[:, None] * inv[None, :]
    reps = (LANES // 2) // (dim // 2)
    cos = jnp.tile(jnp.cos(ang), (1, 2 * reps))
    sin = jnp.tile(jnp.sin(ang), (1, reps))
    sin = jnp.concatenate([-sin, sin], axis=1)
    return cos, sin, cos * scale, sin * scale


def _pair_interleave(a0, b0, half_dim):
    r = np.arange(half_dim)
    return np.concatenate([a0 + r, b0 + r, a0 + half_dim + r, b0 + half_dim + r])


def _cast_w(w, cols=None):
    if cols is not None:
        w = w[:, cols]
    return w.astype(BF16)


def _conv_layer(x, shift, scale, gate, g, w_in, conv_k, w_out, tm):
    b, s, d = x.shape
    e = w_out.shape[0]
    ops = [("mul", (e, 2 * e), e, [(0, 0, 1)], None, 1),
           ("mulsilu", (0, 3 * e), e, [(1, 0, 1)], None, 1)]
    x, (t, gz) = _inproj(x, shift, scale, g, _cast_w(w_in), [], [(e, 1), (e, 1)], ops, tm)
    halo = 16
    hb = tm // halo
    last = s // halo - 1
    specs = [
        pl.BlockSpec((1, halo, e), lambda bi, i: (bi, jnp.maximum(i * hb - 1, 0), 0)),
        _tile_spec(tm, e),
        pl.BlockSpec((1, halo, e), lambda bi, i: (bi, jnp.minimum((i + 1) * hb, last), 0)),
        _tile_spec(tm, e),
        _resident(conv_k.shape),
    ] + _outproj_common_specs(tm, d)
    return _Residual(functools.partial(_conv_residual, halo), specs,
                     (t, t, t, gz, conv_k, _cast_w(w_out), x, gate),
                     [pltpu.VMEM((tm + 2 * halo, e), F32)], (b, s, d))


def _dil_layer(x, shift, scale, gate, g, w_in, w_out, tm):
    if isinstance(x, _Residual):
        x = _outproj_call(x, tm)
    b, s, d = x.shape
    n_grp = len(DIL_RATES)
    e = w_out.shape[0]
    dh = e // DIL_HEADS
    cos, sin, cos_q, sin_q = _rope_tables(s, dh, dh ** -0.5)
    outs = [(e, r) for r in DIL_RATES] * 3 + [(e, 1)]
    ops, tables = [], []
    for gi, r in enumerate(DIL_RATES):
        grouped = lambda t: t if r == 1 else t.reshape(s // r, r, LANES).transpose(1, 0, 2)
        tables += [grouped(t) for t in (cos, sin, cos_q, sin_q)]
        ops.append(("rope", (gi * e,), e, [(gi, 0, r)], (4 * gi + 2, 4 * gi + 3), r))
        ops.append(("rope", ((n_grp + gi) * e,), e, [(n_grp + gi, 0, r)], (4 * gi, 4 * gi + 1), r))
    ops.append(("plain", (2 * n_grp * e,), e,
                [(2 * n_grp + gi, 0, r) for gi, r in enumerate(DIL_RATES)], None, 1))
    ops.append(("silu", (2 * n_grp * e + e,), e, [(3 * n_grp, 0, 1)], None, 1))
    x, res = _inproj(x, shift, scale, g, _cast_w(w_in), tables, outs, ops, tm)
    zs = res[3 * n_grp]
    os_, lses = [], []
    for gi, rate in enumerate(DIL_RATES):
        half = DIL_WINDOWS[gi] // (2 * rate)
        grouped = lambda t: t.reshape(b, rate, s // rate, e)
        o, lse = _dil_attention(grouped(res[gi]), grouped(res[n_grp + gi]),
                                grouped(res[2 * n_grp + gi]), half, 512)
        os_.append(o)
        lses.append(lse)
    rows = np.arange(2 * LANES)[:, None] % LANES
    expand = jnp.asarray(rows == (np.arange(e)[None, :] // dh), dtype=BF16)
    grouped_spec = lambda r, wd: pl.BlockSpec((1, r, tm // r, wd), lambda bi, i: (bi, 0, i, 0))
    specs = ([grouped_spec(r, e) for r in DIL_RATES] + [grouped_spec(r, LANES) for r in DIL_RATES]
             + [_tile_spec(tm, e), _resident(expand.shape)] + _outproj_common_specs(tm, d))
    scratch = ([pltpu.VMEM((e // LANES, tm, LANES), F32)] * n_grp
               + [pltpu.VMEM((1, tm, LANES), F32)] * n_grp)
    return _Residual(functools.partial(_dil_residual, n_grp), specs,
                     (*os_, *lses, zs, expand, _cast_w(w_out), x, gate), scratch, (b, s, d))


def _swa_layer(x, shift, scale, gate, g, w_in, sink, w_out, tm):
    b, s, d = x.shape
    hq, hk, dh = SWA_Q_HEADS, SWA_KV_HEADS, SWA_HEAD_DIM
    grp = hq // hk
    pairs = [((2 * p) * grp + gi, (2 * p + 1) * grp + gi) for p in range(hk // 2) for gi in range(grp)]
    q_cols = np.concatenate([_pair_interleave(a * dh, b_ * dh, dh // 2) for a, b_ in pairs])
    k_cols = hq * dh + np.concatenate(
        [_pair_interleave(2 * p * dh, (2 * p + 1) * dh, dh // 2) for p in range(hk // 2)])
    v_cols = hq * dh + hk * dh + np.arange(hk * dh)
    head_cols = np.concatenate([np.concatenate([a * dh + np.arange(dh), b_ * dh + np.arange(dh)])
                                for a, b_ in pairs])
    z_cols = hq * dh + 2 * hk * dh + head_cols
    w = _cast_w(w_in, np.concatenate([q_cols, k_cols, v_cols, z_cols]))
    sink_pairs = sink[np.array([h for pr in pairs for h in pr])] * math.log2(math.e)
    cos, sin, cos_q, sin_q = _rope_tables(s, dh, dh ** -0.5 * math.log2(math.e))
    nq, nk = hq * dh, hk * dh
    ops = [("rope", (0,), nq, [(0, 0, 1)], (2, 3), 1),
           ("rope", (nq,), nk, [(1, 0, 1)], (0, 1), 1),
           ("plain", (nq + nk,), nk, [(2, 0, 1)], None, 1),
           ("silu", (nq + 2 * nk,), nq, [(3, 0, 1)], None, 1)]
    x, (q, k, v, zs) = _inproj(x, shift, scale, g, w, [cos, sin, cos_q, sin_q],
                               [(nq, 1), (nk, 1), (nk, 1), (nq, 1)], ops, tm)
    o = _swa_attention(q, k, v, sink_pairs, SWA_HALF, 512, grp)
    specs = [_tile_spec(tm, nq), _tile_spec(tm, nq)] + _outproj_common_specs(tm, d)
    return _Residual(_mul_residual, specs, (o, zs, _cast_w(w_out[head_cols]), x, gate), [], (b, s, d))


def _diff_layer(x, shift, scale, gate, g, w_in, lam_vecs, subln_g, w_out, layer_idx, tm):
    b, s, d = x.shape
    nh, dh = DIFF_HEADS, DIFF_HEAD_DIM
    e = nh * 2 * dh
    qk_cols = np.concatenate([_pair_interleave(h * 2 * dh, h * 2 * dh + dh, dh // 2) for h in range(nh)])
    cols = np.concatenate([qk_cols, e + qk_cols, 2 * e + np.arange(2 * e)])
    w = _cast_w(w_in, cols)
    cos, sin, cos_q, sin_q = _rope_tables(s, dh, dh ** -0.5 * math.log2(math.e))
    ops = [("rope", (0,), e, [(0, 0, 1)], (2, 3), 1),
           ("rope", (e,), e, [(1, 0, 1)], (0, 1), 1),
           ("plain", (2 * e,), e, [(2, 0, 1)], None, 1),
           ("silu", (3 * e,), e, [(3, 0, 1)], None, 1)]
    x, (q, k, v, zs) = _inproj(x, shift, scale, g, w, [cos, sin, cos_q, sin_q], [(e, 1)] * 4, ops, tm)
    lam_init = 0.8 - 0.6 * math.exp(-0.3 * layer_idx)
    o = _diff_attention(q, k, v, lam_vecs, subln_g, lam_init, 512, 1024)
    specs = [_tile_spec(tm, e), _tile_spec(tm, e)] + _outproj_common_specs(tm, d)
    return _Residual(_mul_residual, specs, (o, zs, _cast_w(w_out), x, gate), [], (b, s, d))


def kernel(x, c, norm_g, w_mod, b_mod, conv_w_in, conv_k, conv_w_out, dil_w_in, dil_w_out,
           swa_w_in, swa_sink, swa_w_out, diff_w_in, diff_lambda, diff_subln_g, diff_w_out,
           final_g):
    b, s, d = x.shape
    depth = norm_g.shape[0]
    tm = min(512, s)
    mod = _modulation(c, w_mod, b_mod)
    for i in range(depth):
        kind, j = i % N_MIXERS, i // N_MIXERS
        shift, scale, gate = (mod[i, t][:, None, :] for t in range(3))
        g = norm_g[i].reshape(1, d)
        if kind == 0:
            x = _conv_layer(x, shift, scale, gate, g, conv_w_in[j], conv_k[j], conv_w_out[j], tm)
        elif kind == 1:
            x = _dil_layer(x, shift, scale, gate, g, dil_w_in[j], dil_w_out[j], tm)
        elif kind == 2:
            x = _swa_layer(x, shift, scale, gate, g, swa_w_in[j], swa_sink[j], swa_w_out[j], tm)
        else:
            x = _diff_layer(x, shift, scale, gate, g, diff_w_in[j], diff_lambda[j], diff_subln_g[j],
                            diff_w_out[j], i, tm)
    return _outproj_call(x, tm, final_g)
```
